```python
import jax, jax.numpy as jnp
from jax import lax
import numpy as np


D_MODEL = 2048
BATCH = 8
SEQ = 2048
DEPTH = 2

CHUNK = 64
N_MIXERS = 2
GLA_HEADS = 4
GLA_DK = D_MODEL // 2 // GLA_HEADS
GLA_DV = D_MODEL // GLA_HEADS
GLA_GATE_RANK = 16
GLA_TAU = 16.0
GLA_QK = GLA_HEADS * GLA_DK
GLA_IN_COLS = 2 * GLA_QK + 2 * D_MODEL + GLA_GATE_RANK
ATT_HEADS = 16
ATT_HD = D_MODEL // ATT_HEADS
LEFT_CHUNKS = 8
BAND_CHUNKS = LEFT_CHUNKS + 1
BAND = BAND_CHUNKS * CHUNK
REL_CLIP = 256
REL_SIZE = REL_CLIP + CHUNK
D_FF = 4 * D_MODEL
EPS = 1e-6
N_GLA = (DEPTH + 1) // 2
N_ATT = DEPTH // 2

kernel_name = 'hybrid_gla_chunk_relattn_encoder'


def rmsnorm(x, g):
    xf = x.astype(jnp.float32)
    y = xf * lax.rsqrt(jnp.mean(xf * xf, axis=-1, keepdims=True) + EPS)
    return (y * g.astype(jnp.float32)).astype(x.dtype)


def gla_mixer(h, w_in, w_gate_up, b_gate, g_out, w_out):
    B, S, _ = h.shape
    N = S // CHUNK
    f32 = jnp.float32
    proj = h @ w_in
    q, k, v, r, z = jnp.split(proj, [GLA_QK, 2 * GLA_QK, 2 * GLA_QK + D_MODEL, 2 * GLA_QK + 2 * D_MODEL], axis=-1)
    logit = (z @ w_gate_up + b_gate).astype(f32)
    log_a = jax.nn.log_sigmoid(logit) / GLA_TAU

    def heads(t, d):
        return t.astype(f32).reshape(B, N, CHUNK, GLA_HEADS, d).transpose(0, 1, 3, 2, 4)

    qh = heads(q, GLA_DK) * (GLA_DK ** -0.5)
    kh = heads(k, GLA_DK)
    vh = heads(v, GLA_DV)
    b = jnp.cumsum(heads(log_a, GLA_DK), axis=3)
    b_last = b[:, :, :, -1:, :]
    q_dec = qh * jnp.exp(b)
    k_inv = kh * jnp.exp(-b)
    k_end = kh * jnp.exp(b_last - b)
    chunk_decay = jnp.exp(b_last[:, :, :, 0, :])

    causal = jnp.tril(jnp.ones((CHUNK, CHUNK), dtype=bool))
    A = jnp.where(causal, jnp.einsum('bnhcd,bnhsd->bnhcs', q_dec, k_inv), 0.0)
    o_intra = jnp.einsum('bnhcs,bnhsv->bnhcv', A, vh)

    def step(state, inp):
        q_n, k_n, v_n, dec_n = inp
        o_n = jnp.einsum('bhcd,bhdv->bhcv', q_n, state)
        state = dec_n[..., None] * state + jnp.einsum('bhcd,bhcv->bhdv', k_n, v_n)
        return state, o_n

    s0 = jnp.zeros((B, GLA_HEADS, GLA_DK, GLA_DV), f32)
    mv = lambda t: jnp.moveaxis(t, 1, 0)
    _, o_inter = lax.scan(step, s0, (mv(q_dec), mv(k_end), mv(vh), mv(chunk_decay)))
    o = o_intra + jnp.moveaxis(o_inter, 0, 1)
    o = o.transpose(0, 1, 3, 2, 4).reshape(B, S, GLA_HEADS, GLA_DV)
    o = o * lax.rsqrt(jnp.mean(o * o, axis=-1, keepdims=True) + EPS)
    o = o.reshape(B, S, D_MODEL) * g_out.astype(f32)
    o = o * jax.nn.silu(r.astype(f32))
    return (o.astype(h.dtype) @ w_out).astype(h.dtype)


def chunk_relattn_mixer(h, w_in, g_q, g_k, rel_bias, w_out):
    B, S, _ = h.shape
    N = S // CHUNK
    f32 = jnp.float32
    q, k, v = jnp.split(h @ w_in, 3, axis=-1)
    q = rmsnorm(q.reshape(B, S, ATT_HEADS, ATT_HD), g_q) * (ATT_HD ** -0.5)
    k = rmsnorm(k.reshape(B, S, ATT_HEADS, ATT_HD), g_k)
    v = v.reshape(B, S, ATT_HEADS, ATT_HD)
    pad = LEFT_CHUNKS * CHUNK
    kp = jnp.pad(k, ((0, 0), (pad, 0), (0, 0), (0, 0)))
    vp = jnp.pad(v, ((0, 0), (pad, 0), (0, 0), (0, 0)))
    c_idx = jnp.arange(CHUNK)[:, None]
    m_idx = jnp.arange(BAND)[None, :]
    dist = pad + c_idx - m_idx
    rel_idx = jnp.clip(dist, -(CHUNK - 1), REL_CLIP) + (CHUNK - 1)
    bias = rel_bias[:, rel_idx].astype(f32)
    key_pos = jnp.arange(BAND)
    qc = jnp.moveaxis(q.reshape(B, N, CHUNK, ATT_HEADS, ATT_HD), 1, 0)

    def one_chunk(args):
        q_n, n = args
        k_n = lax.dynamic_slice_in_dim(kp, n * CHUNK, BAND, axis=1)
        v_n = lax.dynamic_slice_in_dim(vp, n * CHUNK, BAND, axis=1)
        s = jnp.einsum('bchd,bmhd->bhcm', q_n, k_n).astype(f32) + bias
        valid = key_pos >= (LEFT_CHUNKS - n) * CHUNK
        s = jnp.where(valid, s, -jnp.inf)
        p = jax.nn.softmax(s, axis=-1)
        return jnp.einsum('bhcm,bmhd->bchd', p.astype(v_n.dtype), v_n)

    out = lax.map(one_chunk, (qc, jnp.arange(N)))
    out = jnp.moveaxis(out, 0, 1).reshape(B, S, D_MODEL)
    return (out @ w_out).astype(h.dtype)


def sqrelu_mlp(h, w_up, w_down):
    return jnp.square(jax.nn.relu(h @ w_up)) @ w_down


def setup_inputs(seed: int = 0) -> dict:
    key = jax.random.key(seed)
    ks = jax.random.split(key, 16)
    f32 = jnp.float32

    def nrm(k, shape, scale):
        return jax.random.normal(k, shape, f32) * scale

    return {
        'x': nrm(ks[0], (BATCH, SEQ, D_MODEL), 1.0),
        'norm_mix_g': 1.0 + nrm(ks[1], (DEPTH, D_MODEL), 0.02),
        'norm_mlp_g': 1.0 + nrm(ks[2], (DEPTH, D_MODEL), 0.02),
        'gla_w_in': nrm(ks[3], (N_GLA, D_MODEL, GLA_IN_COLS), D_MODEL ** -0.5),
        'gla_w_gate_up': nrm(ks[4], (N_GLA, GLA_GATE_RANK, GLA_QK), GLA_GATE_RANK ** -0.5),
        'gla_b_gate': nrm(ks[5], (N_GLA, GLA_QK), 0.1),
        'gla_g_out': 1.0 + nrm(ks[6], (N_GLA, D_MODEL), 0.02),
        'gla_w_out': nrm(ks[7], (N_GLA, D_MODEL, D_MODEL), D_MODEL ** -0.5),
        'att_w_in': nrm(ks[8], (N_ATT, D_MODEL, 3 * D_MODEL), D_MODEL ** -0.5),
        'att_g_q': 1.0 + nrm(ks[9], (N_ATT, ATT_HD), 0.02),
        'att_g_k': 1.0 + nrm(ks[10], (N_ATT, ATT_HD), 0.02),
        'att_rel_bias': nrm(ks[11], (N_ATT, ATT_HEADS, REL_SIZE), 0.2),
        'att_w_out': nrm(ks[12], (N_ATT, D_MODEL, D_MODEL), D_MODEL ** -0.5),
        'mlp_w_up': nrm(ks[13], (DEPTH, D_MODEL, D_FF), D_MODEL ** -0.5),
        'mlp_w_down': nrm(ks[14], (DEPTH, D_FF, D_MODEL), D_FF ** -0.5),
    }


def reference(x, norm_mix_g, norm_mlp_g, gla_w_in, gla_w_gate_up, gla_b_gate, gla_g_out, gla_w_out,
              att_w_in, att_g_q, att_g_k, att_rel_bias, att_w_out, mlp_w_up, mlp_w_down):
    for i in range(DEPTH):
        h = rmsnorm(x, norm_mix_g[i])
        j = i // N_MIXERS
        if i % N_MIXERS == 0:
            mix = gla_mixer(h, gla_w_in[j], gla_w_gate_up[j], gla_b_gate[j], gla_g_out[j], gla_w_out[j])
        else:
            mix = chunk_relattn_mixer(h, att_w_in[j], att_g_q[j], att_g_k[j], att_rel_bias[j], att_w_out[j])
        x = x + mix.astype(x.dtype)
        x = x + sqrelu_mlp(rmsnorm(x, norm_mlp_g[i]), mlp_w_up[i], mlp_w_down[i]).astype(x.dtype)
    return x
```

```python
import functools

import jax
import jax.numpy as jnp
from jax import lax
from jax.experimental import pallas as pl
from jax.experimental.pallas import tpu as pltpu

F32 = jnp.float32
BF16 = jnp.bfloat16

EPS = 1e-6
CHUNK = 64
N_MIXERS = 2
GLA_HEADS = 4
GLA_GATE_RANK = 16
GLA_TAU = 16.0
ATT_HEADS = 16
LEFT_CHUNKS = 8
REL_CLIP = 256

LANES = 128
NORM_ROWS = 32
VMEM_LIMIT = 56 * 1024 * 1024
NEG = -1e30

GLA_ROWS = 256
ATT_QB = 256
ATT_KB = ATT_QB + LEFT_CHUNKS * CHUNK
ATT_HB = 8


def _params(*sem):
    return pltpu.CompilerParams(dimension_semantics=sem, vmem_limit_bytes=VMEM_LIMIT)


def _rmsnorm_rows_to(x_ref, g_ref, dst_ref, copy_ref=None):
    rows = x_ref.shape[0]

    def body(r, carry):
        sl = pl.ds(pl.multiple_of(r * NORM_ROWS, NORM_ROWS), NORM_ROWS)
        x = x_ref[sl, :]
        ms = jnp.mean(x * x, axis=-1, keepdims=True)
        dst_ref[sl, :] = ((x * lax.rsqrt(ms + EPS)) * g_ref[...]).astype(dst_ref.dtype)
        if copy_ref is not None:
            copy_ref[sl, :] = x
        return carry

    lax.fori_loop(0, rows // NORM_ROWS, body, 0)


def _norm_matmul_kernel(x_ref, g_ref, w_ref, o_ref, xn_ref):
    @pl.when(pl.program_id(1) == 0)
    def _():
        _rmsnorm_rows_to(x_ref, g_ref, xn_ref)

    o_ref[...] = jnp.dot(xn_ref[...], w_ref[...], preferred_element_type=F32).astype(o_ref.dtype)


def _norm_matmul_z_kernel(x_ref, g_ref, w_ref, wz_ref, o_ref, z_ref, xn_ref):
    @pl.when(pl.program_id(1) == 0)
    def _():
        _rmsnorm_rows_to(x_ref, g_ref, xn_ref)
        z_ref[...] = jnp.dot(xn_ref[...], wz_ref[...], preferred_element_type=F32)

    o_ref[...] = jnp.dot(xn_ref[...], w_ref[...], preferred_element_type=F32).astype(o_ref.dtype)


def _norm_matmul(x, g, w, wz=None, *, tm=1024, tn=1024):
    t, d = x.shape
    n = w.shape[1]
    tm, tn = min(tm, t), min(tn, n)
    grid = (t // tm, n // tn)
    x_spec = pl.BlockSpec((tm, d), lambda i, j: (i, 0))
    g_spec = pl.BlockSpec((1, d), lambda i, j: (0, 0))
    w_spec = pl.BlockSpec((d, tn), lambda i, j: (0, j))
    o_spec = pl.BlockSpec((tm, tn), lambda i, j: (i, j))
    scratch = [pltpu.VMEM((tm, d), BF16)]
    if wz is None:
        return pl.pallas_call(
            _norm_matmul_kernel,
            out_shape=jax.ShapeDtypeStruct((t, n), BF16),
            grid=grid, in_specs=[x_spec, g_spec, w_spec], out_specs=o_spec,
            scratch_shapes=scratch, compiler_params=_params("parallel", "arbitrary"),
            name="norm_matmul",
        )(x, g, w)
    nz = wz.shape[1]
    wz_spec = pl.BlockSpec((d, nz), lambda i, j: (0, 0))
    z_spec = pl.BlockSpec((tm, nz), lambda i, j: (i, 0))
    return pl.pallas_call(
        _norm_matmul_z_kernel,
        out_shape=(jax.ShapeDtypeStruct((t, n), BF16), jax.ShapeDtypeStruct((t, nz), F32)),
        grid=grid, in_specs=[x_spec, g_spec, w_spec, wz_spec], out_specs=(o_spec, z_spec),
        scratch_shapes=scratch, compiler_params=_params("parallel", "arbitrary"),
        name="norm_matmul_z",
    )(x, g, w, wz)


def _matmul_res_kernel(a_ref, w_ref, r_ref, o_ref):
    o_ref[...] = r_ref[...] + jnp.dot(a_ref[...], w_ref[...], preferred_element_type=F32)


def _matmul_res(a, w, res, *, tm=1024, tn=1024):
    t, k = a.shape
    n = w.shape[1]
    tm, tn = min(tm, t), min(tn, n)
    return pl.pallas_call(
        _matmul_res_kernel,
        out_shape=jax.ShapeDtypeStruct((t, n), F32),
        grid=(t // tm, n // tn),
        in_specs=[pl.BlockSpec((tm, k), lambda i, j: (i, 0)),
                  pl.BlockSpec((k, tn), lambda i, j: (0, j)),
                  pl.BlockSpec((tm, tn), lambda i, j: (i, j))],
        out_specs=pl.BlockSpec((tm, tn), lambda i, j: (i, j)),
        compiler_params=_params("parallel", "parallel"),
        name="matmul_res",
    )(a, w, res)


def _mlp_kernel(x_ref, g_ref, wu_ref, wd_ref, o_ref, xn_ref):
    @pl.when(pl.program_id(1) == 0)
    def _():
        _rmsnorm_rows_to(x_ref, g_ref, xn_ref, copy_ref=o_ref)

    h = jnp.dot(xn_ref[...], wu_ref[...], preferred_element_type=F32)
    h = jnp.square(jnp.maximum(h, 0.0)).astype(BF16)
    o_ref[...] += jnp.dot(h, wd_ref[...], preferred_element_type=F32)


def _mlp(x, g, w_up, w_down, *, tm=512, tf=1024):
    t, d = x.shape
    f = w_up.shape[1]
    tm, tf = min(tm, t), min(tf, f)
    return pl.pallas_call(
        _mlp_kernel,
        out_shape=jax.ShapeDtypeStruct((t, d), F32),
        grid=(t // tm, f // tf),
        in_specs=[pl.BlockSpec((tm, d), lambda i, j: (i, 0)),
                  pl.BlockSpec((1, d), lambda i, j: (0, 0)),
                  pl.BlockSpec((d, tf), lambda i, j: (0, j)),
                  pl.BlockSpec((tf, d), lambda i, j: (j, 0))],
        out_specs=pl.BlockSpec((tm, d), lambda i, j: (i, 0)),
        scratch_shapes=[pltpu.VMEM((tm, d), BF16)],
        compiler_params=_params("parallel", "arbitrary"),
        name="mlp",
    )(x, g, w_up, w_down)


def _split_bf16(a):
    hi = a.astype(BF16)
    lo = (a - hi.astype(F32)).astype(BF16)
    return hi, lo


def _dot(a, b):
    return jnp.dot(a, b, preferred_element_type=F32)


def _dot_tn(a, b):
    return lax.dot_general(a, b, (((0,), (0,)), ((), ())), preferred_element_type=F32)


def _dot_nt(a, b):
    return lax.dot_general(a, b, (((1,), (1,)), ((), ())), preferred_element_type=F32)


def _gla_kernel(q_ref, k_ref, v_ref, r_ref, z_ref, wg_ref, bg_ref, go_ref, o_ref, state_ref):
    rows, dk = q_ref.shape
    dv = v_ref.shape[1]

    @pl.when(pl.program_id(2) == 0)
    def _():
        state_ref[...] = jnp.zeros_like(state_ref)

    z_hi, z_lo = _split_bf16(z_ref[...])
    wg_hi, wg_lo = _split_bf16(wg_ref[...])
    logit = _dot(z_hi, wg_hi) + _dot(z_lo, wg_hi) + _dot(z_hi, wg_lo) + bg_ref[...]
    log_a = (jnp.minimum(logit, 0.0) - jnp.log1p(jnp.exp(-jnp.abs(logit)))) * (1.0 / GLA_TAU)

    row = lax.broadcasted_iota(jnp.int32, (rows, rows), 0)
    col = lax.broadcasted_iota(jnp.int32, (rows, rows), 1)
    tri = ((row // CHUNK == col // CHUNK) & (col <= row)).astype(BF16)
    la_hi, la_lo = _split_bf16(log_a)
    b = _dot(tri, la_hi) + _dot(tri, la_lo)

    q = q_ref[...].astype(F32) * (dk ** -0.5)
    k = k_ref[...].astype(F32)
    q_dec = (q * jnp.exp(b)).astype(BF16)
    k_inv = (k * jnp.exp(-b)).astype(BF16)

    crow = lax.broadcasted_iota(jnp.int32, (CHUNK, CHUNK), 0)
    ccol = lax.broadcasted_iota(jnp.int32, (CHUNK, CHUNK), 1)
    causal = ccol <= crow
    ones = jnp.ones((CHUNK, LANES), BF16)

    for c in range(rows // CHUNK):
        sl = slice(c * CHUNK, (c + 1) * CHUNK)
        b_c = b[sl]
        b_last = b_c[CHUNK - 1:CHUNK, :]
        k_end = (k[sl] * jnp.exp(b_last - b_c)).astype(BF16)
        dec_col = jnp.exp(_dot_tn(la_hi[sl], ones) + _dot_tn(la_lo[sl], ones))
        v_c = v_ref[sl, :]
        a = jnp.where(causal, _dot_nt(q_dec[sl], k_inv[sl]), 0.0).astype(BF16)
        state = state_ref[...]
        o = _dot(a, v_c) + _dot(q_dec[sl], state.astype(BF16))
        kv = _dot_tn(k_end, v_c)
        for j in range(dv // LANES):
            ls = slice(j * LANES, (j + 1) * LANES)
            state_ref[:, ls] = state[:, ls] * dec_col + kv[:, ls]
        ms = jnp.mean(o * o, axis=-1, keepdims=True)
        r = r_ref[sl, :].astype(F32)
        y = (o * lax.rsqrt(ms + EPS)) * go_ref[...]
        o_ref[sl, :] = (y * (r * jax.nn.sigmoid(r))).astype(o_ref.dtype)


def _gla_core(proj, z, wg, bg, g_out, batch, seq):
    t = proj.shape[0]
    d = g_out.shape[1]
    dv = d // GLA_HEADS
    dk = dv // 2
    rows = min(GLA_ROWS, seq)
    nblk = seq // rows
    k_off = GLA_HEADS
    v_off = (2 * GLA_HEADS * dk) // dv
    r_off = v_off + GLA_HEADS
    rowmap = lambda b, h, n: b * nblk + n
    return pl.pallas_call(
        _gla_kernel,
        out_shape=jax.ShapeDtypeStruct((t, d), BF16),
        grid=(batch, GLA_HEADS, nblk),
        in_specs=[pl.BlockSpec((rows, dk), lambda b, h, n: (rowmap(b, h, n), h)),
                  pl.BlockSpec((rows, dk), lambda b, h, n: (rowmap(b, h, n), k_off + h)),
                  pl.BlockSpec((rows, dv), lambda b, h, n: (rowmap(b, h, n), v_off + h)),
                  pl.BlockSpec((rows, dv), lambda b, h, n: (rowmap(b, h, n), r_off + h)),
                  pl.BlockSpec((rows, z.shape[1]), lambda b, h, n: (rowmap(b, h, n), 0)),
                  pl.BlockSpec((wg.shape[0], dk), lambda b, h, n: (0, h)),
                  pl.BlockSpec((1, dk), lambda b, h, n: (0, h)),
                  pl.BlockSpec((1, dv), lambda b, h, n: (0, h))],
        out_specs=pl.BlockSpec((rows, dv), lambda b, h, n: (rowmap(b, h, n), h)),
        scratch_shapes=[pltpu.VMEM((dk, dv), F32)],
        compiler_params=_params("parallel", "parallel", "arbitrary"),
        name="gla_core",
    )(proj, proj, proj, proj, z, wg, bg, g_out)


def _attn_kernel(q_ref, k0_ref, k1_ref, k2_ref, v0_ref, v1_ref, v2_ref, gq_ref, gk_ref, bias_ref,
                 o_ref):
    qb = q_ref.shape[0]
    hd = gq_ref.shape[1]
    nkb = ATT_KB // qb
    i = pl.program_id(2)
    k_refs = (k0_ref, k1_ref, k2_ref)
    v_refs = (v0_ref, v1_ref, v2_ref)
    pad = [jnp.where(i >= nkb - 1 - j, 0.0, NEG) for j in range(nkb)]

    def qk_norm(t, g_ref):
        t = t.astype(F32)
        ms = jnp.mean(t * t, axis=-1, keepdims=True)
        return (t * lax.rsqrt(ms + EPS)) * g_ref[...]

    for h in range(q_ref.shape[1] // hd):
        cs = slice(h * hd, (h + 1) * hd)
        q = (qk_norm(q_ref[:, cs], gq_ref) * (hd ** -0.5)).astype(BF16)
        s = []
        for j in range(nkb):
            k = qk_norm(k_refs[j][:, cs], gk_ref).astype(BF16)
            s.append(_dot_nt(q, k) + bias_ref[h, :, j * qb:(j + 1) * qb] + pad[j])
        m = functools.reduce(jnp.maximum, [jnp.max(sj, axis=-1, keepdims=True) for sj in s])
        p = [jnp.exp(sj - m) for sj in s]
        l = functools.reduce(jnp.add, [jnp.sum(pj, axis=-1, keepdims=True) for pj in p])
        o = functools.reduce(jnp.add, [_dot(p[j].astype(BF16), v_refs[j][:, cs]) for j in range(nkb)])
        o_ref[:, cs] = (o / l).astype(o_ref.dtype)


def _attn_bias(rel_bias, qb):
    left = LEFT_CHUNKS * CHUNK
    c = jnp.arange(qb)[:, None]
    m = jnp.arange(qb + left)[None, :]
    dist = left + c - m
    idx = jnp.clip(dist, -(CHUNK - 1), REL_CLIP) + (CHUNK - 1)
    band = (m // CHUNK >= c // CHUNK) & (m // CHUNK <= c // CHUNK + LEFT_CHUNKS)
    return jnp.where(band[None], rel_bias[:, idx].astype(F32), NEG)


def _attn_core(qkv, g_q, g_k, rel_bias, batch, seq):
    t = qkv.shape[0]
    hd = g_q.shape[1]
    d = ATT_HEADS * hd
    qb = ATT_QB
    assert seq % qb == 0 and ATT_KB % qb == 0 and ATT_KB // qb == 3
    nq = seq // qb
    hb = ATT_HB
    ng = ATT_HEADS // hb
    bias = _attn_bias(rel_bias, qb)
    w = hb * hd

    def kv_spec(off, back):
        return pl.BlockSpec((qb, w), lambda g, b, i: (b * nq + jnp.maximum(i - back, 0), off * ng + g))

    return pl.pallas_call(
        _attn_kernel,
        out_shape=jax.ShapeDtypeStruct((t, d), BF16),
        grid=(ng, batch, nq),
        in_specs=[pl.BlockSpec((qb, w), lambda g, b, i: (b * nq + i, g)),
                  kv_spec(1, 2), kv_spec(1, 1), kv_spec(1, 0),
                  kv_spec(2, 2), kv_spec(2, 1), kv_spec(2, 0),
                  pl.BlockSpec((1, hd), lambda g, b, i: (0, 0)),
                  pl.BlockSpec((1, hd), lambda g, b, i: (0, 0)),
                  pl.BlockSpec((hb, qb, ATT_KB), lambda g, b, i: (g, 0, 0))],
        out_specs=pl.BlockSpec((qb, w), lambda g, b, i: (b * nq + i, g)),
        compiler_params=_params("parallel", "parallel", "arbitrary"),
        name="attn_core",
    )(qkv, qkv, qkv, qkv, qkv, qkv, qkv, g_q, g_k, bias)


def kernel(x, norm_mix_g, norm_mlp_g, gla_w_in, gla_w_gate_up, gla_b_gate, gla_g_out, gla_w_out,
           att_w_in, att_g_q, att_g_k, att_rel_bias, att_w_out, mlp_w_up, mlp_w_down):
    batch, seq, d = x.shape
    depth = norm_mix_g.shape[0]
    xf = x.reshape(batch * seq, d)
    for i in range(depth):
        j = i // N_MIXERS
        g_mix = norm_mix_g[i].reshape(1, d)
        if i % N_MIXERS == 0:
            w_in = gla_w_in[j]
            n_main = w_in.shape[1] - GLA_GATE_RANK
            w_main = w_in[:, :n_main].astype(BF16)
            w_z = jnp.pad(w_in[:, n_main:], ((0, 0), (0, LANES - GLA_GATE_RANK))).astype(BF16)
            wg = jnp.pad(gla_w_gate_up[j], ((0, LANES - GLA_GATE_RANK), (0, 0)))
            proj, z = _norm_matmul(xf, g_mix, w_main, w_z)
            mix = _gla_core(proj, z, wg, gla_b_gate[j].reshape(1, -1), gla_g_out[j].reshape(1, d),
                            batch, seq)
            w_out = gla_w_out[j]
        else:
            qkv = _norm_matmul(xf, g_mix, att_w_in[j].astype(BF16))
            mix = _attn_core(qkv, att_g_q[j].reshape(1, -1), att_g_k[j].reshape(1, -1),
                             att_rel_bias[j], batch, seq)
            w_out = att_w_out[j]
        xf = _matmul_res(mix, w_out.astype(BF16), xf)
        xf = _mlp(xf, norm_mlp_g[i].reshape(1, d), mlp_w_up[i].astype(BF16),
                  mlp_w_down[i].astype(BF16))
    return xf.reshape(batch, seq, d)
```

```python
import functools

import jax
import jax.numpy as jnp
from jax import lax
from jax.experimental import pallas as pl
from jax.experimental.pallas import tpu as pltpu

F32 = jnp.float32
BF16 = jnp.bfloat16

EPS = 1e-6
CHUNK = 64
N_MIXERS = 2
GLA_HEADS = 4
GLA_GATE_RANK = 16
GLA_TAU = 16.0
ATT_HEADS = 16
LEFT_CHUNKS = 8
REL_CLIP = 256

LANES = 128
NORM_ROWS = 32
NORM_UNROLL = 4
VMEM_LIMIT = 56 * 1024 * 1024
NEG = -1e30
LOG2E = 1.4426950408889634

GLA_ROWS = 256
ATT_QB = 256
ATT_KB = ATT_QB + LEFT_CHUNKS * CHUNK
ATT_HB = 8


def _params(*sem):
    return pltpu.CompilerParams(dimension_semantics=sem, vmem_limit_bytes=VMEM_LIMIT)


def _rmsnorm_rows_to(x_ref, g_ref, dst_ref, copy_ref=None):
    rows = x_ref.shape[0]

    def body(r, carry):
        sl = pl.ds(pl.multiple_of(r * NORM_ROWS, NORM_ROWS), NORM_ROWS)
        x = x_ref[sl, :]
        ms = jnp.mean(x * x, axis=-1, keepdims=True)
        dst_ref[sl, :] = ((x * lax.rsqrt(ms + EPS)) * g_ref[...]).astype(dst_ref.dtype)
        if copy_ref is not None:
            copy_ref[sl, :] = x
        return carry

    lax.fori_loop(0, rows // NORM_ROWS, body, 0, unroll=NORM_UNROLL)


def _norm_matmul_headnorm_kernel(x_ref, g_ref, w_ref, gh_ref, o_ref, xn_ref, *, norm_blocks, hd):
    j = pl.program_id(1)

    @pl.when(j == 0)
    def _():
        _rmsnorm_rows_to(x_ref, g_ref, xn_ref)

    acc = jnp.dot(xn_ref[...], w_ref[...], preferred_element_type=F32)
    normed = j < norm_blocks
    for h in range(acc.shape[1] // hd):
        cs = slice(h * hd, (h + 1) * hd)
        a = acc[:, cs]
        ms = jnp.mean(a * a, axis=-1, keepdims=True)
        scale = jnp.where(normed, lax.rsqrt(ms + EPS), 1.0)
        o_ref[:, cs] = ((a * scale) * gh_ref[:, cs]).astype(o_ref.dtype)


def _norm_matmul_z_kernel(x_ref, g_ref, w_ref, wz_ref, o_ref, z_ref, xn_ref):
    @pl.when(pl.program_id(1) == 0)
    def _():
        _rmsnorm_rows_to(x_ref, g_ref, xn_ref)
        z_ref[...] = jnp.dot(xn_ref[...], wz_ref[...], preferred_element_type=F32)

    o_ref[...] = jnp.dot(xn_ref[...], w_ref[...], preferred_element_type=F32).astype(o_ref.dtype)


def _norm_matmul(x, g, w, wz=None, gh=None, hd=None, n_norm=0, *, tm=1024, tn=1024):
    t, d = x.shape
    n = w.shape[1]
    tm, tn = min(tm, t), min(tn, n)
    grid = (t // tm, n // tn)
    x_spec = pl.BlockSpec((tm, d), lambda i, j: (i, 0))
    g_spec = pl.BlockSpec((1, d), lambda i, j: (0, 0))
    w_spec = pl.BlockSpec((d, tn), lambda i, j: (0, j))
    o_spec = pl.BlockSpec((tm, tn), lambda i, j: (i, j))
    scratch = [pltpu.VMEM((tm, d), BF16)]
    if gh is not None:
        assert n_norm % tn == 0 and tn % hd == 0 and gh.shape[1] == n
        gh_spec = pl.BlockSpec((1, tn), lambda i, j: (0, j))
        return pl.pallas_call(
            functools.partial(_norm_matmul_headnorm_kernel, norm_blocks=n_norm // tn, hd=hd),
            out_shape=jax.ShapeDtypeStruct((t, n), BF16),
            grid=grid, in_specs=[x_spec, g_spec, w_spec, gh_spec], out_specs=o_spec,
            scratch_shapes=scratch, compiler_params=_params("parallel", "arbitrary"),
            name="norm_matmul_headnorm",
        )(x, g, w, gh)
    nz = wz.shape[1]
    wz_spec = pl.BlockSpec((d, nz), lambda i, j: (0, 0))
    z_spec = pl.BlockSpec((tm, nz), lambda i, j: (i, 0))
    return pl.pallas_call(
        _norm_matmul_z_kernel,
        out_shape=(jax.ShapeDtypeStruct((t, n), BF16), jax.ShapeDtypeStruct((t, nz), F32)),
        grid=grid, in_specs=[x_spec, g_spec, w_spec, wz_spec], out_specs=(o_spec, z_spec),
        scratch_shapes=scratch, compiler_params=_params("parallel", "arbitrary"),
        name="norm_matmul_z",
    )(x, g, w, wz)


def _matmul_res_kernel(a_ref, w_ref, r_ref, o_ref):
    o_ref[...] = r_ref[...] + jnp.dot(a_ref[...], w_ref[...], preferred_element_type=F32)


def _matmul_res(a, w, res, *, tm=1024, tn=1024):
    t, k = a.shape
    n = w.shape[1]
    tm, tn = min(tm, t), min(tn, n)
    return pl.pallas_call(
        _matmul_res_kernel,
        out_shape=jax.ShapeDtypeStruct((t, n), F32),
        grid=(t // tm, n // tn),
        in_specs=[pl.BlockSpec((tm, k), lambda i, j: (i, 0)),
                  pl.BlockSpec((k, tn), lambda i, j: (0, j)),
                  pl.BlockSpec((tm, tn), lambda i, j: (i, j))],
        out_specs=pl.BlockSpec((tm, tn), lambda i, j: (i, j)),
        compiler_params=_params("parallel", "parallel"),
        name="matmul_res",
    )(a, w, res)


def _mlp_kernel(x_ref, g_ref, wu_ref, wd_ref, o_ref, xn_ref):
    @pl.when(pl.program_id(1) == 0)
    def _():
        _rmsnorm_rows_to(x_ref, g_ref, xn_ref, copy_ref=o_ref)

    h = jnp.dot(xn_ref[...], wu_ref[...], preferred_element_type=F32)
    h = jnp.square(jnp.maximum(h, 0.0)).astype(BF16)
    o_ref[...] += jnp.dot(h, wd_ref[...], preferred_element_type=F32)


def _mlp(x, g, w_up, w_down, *, tm=512, tf=1024):
    t, d = x.shape
    f = w_up.shape[1]
    tm, tf = min(tm, t), min(tf, f)
    return pl.pallas_call(
        _mlp_kernel,
        out_shape=jax.ShapeDtypeStruct((t, d), F32),
        grid=(t // tm, f // tf),
        in_specs=[pl.BlockSpec((tm, d), lambda i, j: (i, 0)),
                  pl.BlockSpec((1, d), lambda i, j: (0, 0)),
                  pl.BlockSpec((d, tf), lambda i, j: (0, j)),
                  pl.BlockSpec((tf, d), lambda i, j: (j, 0))],
        out_specs=pl.BlockSpec((tm, d), lambda i, j: (i, 0)),
        scratch_shapes=[pltpu.VMEM((tm, d), BF16)],
        compiler_params=_params("parallel", "arbitrary"),
        name="mlp",
    )(x, g, w_up, w_down)


def _split_bf16(a):
    hi = a.astype(BF16)
    lo = (a - hi.astype(F32)).astype(BF16)
    return hi, lo


def _dot(a, b):
    return jnp.dot(a, b, preferred_element_type=F32)


def _dot_tn(a, b):
    return lax.dot_general(a, b, (((0,), (0,)), ((), ())), preferred_element_type=F32)


def _dot_nt(a, b):
    return lax.dot_general(a, b, (((1,), (1,)), ((), ())), preferred_element_type=F32)


def _gla_kernel(q_ref, k_ref, v_ref, r_ref, z_ref, wg_ref, bg_ref, go_ref, o_ref, state_ref):
    rows, dk = q_ref.shape
    dv = v_ref.shape[1]

    @pl.when(pl.program_id(2) == 0)
    def _():
        state_ref[...] = jnp.zeros_like(state_ref)

    z_hi, z_lo = _split_bf16(z_ref[...])
    wg_hi, wg_lo = _split_bf16(wg_ref[...])
    logit = _dot(z_hi, wg_hi) + _dot(z_lo, wg_hi) + _dot(z_hi, wg_lo) + bg_ref[...]
    log_a = (jnp.minimum(logit, 0.0) - jnp.log1p(jnp.exp(-jnp.abs(logit)))) * (1.0 / GLA_TAU)

    row = lax.broadcasted_iota(jnp.int32, (rows, rows), 0)
    col = lax.broadcasted_iota(jnp.int32, (rows, rows), 1)
    tri = ((row // CHUNK == col // CHUNK) & (col <= row)).astype(BF16)
    la_hi, la_lo = _split_bf16(log_a)
    b = _dot(tri, la_hi) + _dot(tri, la_lo)

    q = q_ref[...].astype(F32) * (dk ** -0.5)
    k = k_ref[...].astype(F32)
    q_dec = (q * jnp.exp(b)).astype(BF16)
    k_inv = (k * jnp.exp(-b)).astype(BF16)

    crow = lax.broadcasted_iota(jnp.int32, (CHUNK, CHUNK), 0)
    ccol = lax.broadcasted_iota(jnp.int32, (CHUNK, CHUNK), 1)
    causal = ccol <= crow
    ones = jnp.ones((CHUNK, LANES), BF16)

    for c in range(rows // CHUNK):
        sl = slice(c * CHUNK, (c + 1) * CHUNK)
        b_c = b[sl]
        b_last = b_c[CHUNK - 1:CHUNK, :]
        k_end = (k[sl] * jnp.exp(b_last - b_c)).astype(BF16)
        dec_col = jnp.exp(_dot_tn(la_hi[sl], ones) + _dot_tn(la_lo[sl], ones))
        v_c = v_ref[sl, :]
        a = jnp.where(causal, _dot_nt(q_dec[sl], k_inv[sl]), 0.0).astype(BF16)
        state = state_ref[...]
        o = _dot(a, v_c) + _dot(q_dec[sl], state.astype(BF16))
        kv = _dot_tn(k_end, v_c)
        for j in range(dv // LANES):
            ls = slice(j * LANES, (j + 1) * LANES)
            state_ref[:, ls] = state[:, ls] * dec_col + kv[:, ls]
        ms = jnp.mean(o * o, axis=-1, keepdims=True)
        r = r_ref[sl, :].astype(F32)
        y = (o * lax.rsqrt(ms + EPS)) * go_ref[...]
        o_ref[sl, :] = (y * (r * jax.nn.sigmoid(r))).astype(o_ref.dtype)


def _gla_core(proj, z, wg, bg, g_out, batch, seq):
    t = proj.shape[0]
    d = g_out.shape[1]
    dv = d // GLA_HEADS
    dk = dv // 2
    rows = min(GLA_ROWS, seq)
    nblk = seq // rows
    k_off = GLA_HEADS
    v_off = (2 * GLA_HEADS * dk) // dv
    r_off = v_off + GLA_HEADS
    rowmap = lambda b, h, n: b * nblk + n
    return pl.pallas_call(
        _gla_kernel,
        out_shape=jax.ShapeDtypeStruct((t, d), BF16),
        grid=(batch, GLA_HEADS, nblk),
        in_specs=[pl.BlockSpec((rows, dk), lambda b, h, n: (rowmap(b, h, n), h)),
                  pl.BlockSpec((rows, dk), lambda b, h, n: (rowmap(b, h, n), k_off + h)),
                  pl.BlockSpec((rows, dv), lambda b, h, n: (rowmap(b, h, n), v_off + h)),
                  pl.BlockSpec((rows, dv), lambda b, h, n: (rowmap(b, h, n), r_off + h)),
                  pl.BlockSpec((rows, z.shape[1]), lambda b, h, n: (rowmap(b, h, n), 0)),
                  pl.BlockSpec((wg.shape[0], dk), lambda b, h, n: (0, h)),
                  pl.BlockSpec((1, dk), lambda b, h, n: (0, h)),
                  pl.BlockSpec((1, dv), lambda b, h, n: (0, h))],
        out_specs=pl.BlockSpec((rows, dv), lambda b, h, n: (rowmap(b, h, n), h)),
        scratch_shapes=[pltpu.VMEM((dk, dv), F32)],
        compiler_params=_params("parallel", "parallel", "arbitrary"),
        name="gla_core",
    )(proj, proj, proj, proj, z, wg, bg, g_out)


def _attn_kernel(q_ref, k0_ref, k1_ref, k2_ref, v0_ref, v1_ref, v2_ref, gen_ref, o_ref, bias_ref):
    qb = q_ref.shape[0]
    hb = bias_ref.shape[0]
    hd = q_ref.shape[1] // hb
    nkb = ATT_KB // qb
    cpb = qb // CHUNK
    i = pl.program_id(2)
    k_refs = (k0_ref, k1_ref, k2_ref)
    v_refs = (v0_ref, v1_ref, v2_ref)

    @pl.when((pl.program_id(1) == 0) & (i == 0))
    def _():
        cq = lax.broadcasted_iota(jnp.int32, (qb, qb), 0) // CHUNK
        ck = lax.broadcasted_iota(jnp.int32, (qb, qb), 1) // CHUNK
        for j in range(nkb):
            band = (ck + cpb * j >= cq) & (ck + cpb * j <= cq + LEFT_CHUNKS)
            for h in range(hb):
                gen = jnp.broadcast_to(gen_ref[h, j:j + 1, :], (qb, 2 * qb))
                toe = pltpu.roll(gen, qb + 1, 1, stride=1, stride_axis=0)[:, :qb]
                bias_ref[h, :, j * qb:(j + 1) * qb] = jnp.where(band, toe, NEG)

    ones = jnp.ones((qb, hd), BF16)
    q_ext = jnp.full((qb, hd), 1.0 / hd, BF16)
    k_ext = jnp.concatenate(
        [jnp.full((qb, hd), jnp.where(i >= nkb - 1 - j, 0.0, NEG), F32).astype(BF16)
         for j in range(nkb)], axis=0)
    for h in range(hb):
        cs = slice(h * hd, (h + 1) * hd)
        q = jnp.concatenate([q_ref[:, cs], q_ext], axis=1)
        k = jnp.concatenate([jnp.concatenate([r[:, cs] for r in k_refs], axis=0), k_ext], axis=1)
        s = _dot_nt(q, k) + bias_ref[h]
        m = jnp.max(s, axis=-1, keepdims=True)
        p = jnp.exp2(s - m).astype(BF16)
        v1 = jnp.concatenate([jnp.concatenate([r[:, cs] for r in v_refs], axis=0),
                              jnp.concatenate([ones] * nkb, axis=0)], axis=1)
        acc = _dot(p, v1)
        o_ref[:, cs] = (acc[:, :hd] / acc[:, hd:]).astype(o_ref.dtype)


def _attn_bias_generators(rel_bias, qb):
    left = LEFT_CHUNKS * CHUNK
    x = jnp.arange(2 * qb)[None, :]
    j = jnp.arange((qb + left) // qb)[:, None]
    dist = left - qb * j + (qb - 1) - x
    idx = jnp.clip(dist, -(CHUNK - 1), REL_CLIP) + (CHUNK - 1)
    return rel_bias[:, idx].astype(F32) * LOG2E


def _attn_core(qkv, rel_bias, batch, seq):
    t = qkv.shape[0]
    d = qkv.shape[1] // 3
    hd = d // ATT_HEADS
    qb = ATT_QB
    assert seq % qb == 0 and ATT_KB % qb == 0 and ATT_KB // qb == 3
    nq = seq // qb
    hb = ATT_HB
    ng = ATT_HEADS // hb
    gen = _attn_bias_generators(rel_bias, qb)
    w = hb * hd

    def kv_spec(off, back):
        return pl.BlockSpec((qb, w), lambda g, b, i: (b * nq + jnp.maximum(i - back, 0), off * ng + g))

    return pl.pallas_call(
        _attn_kernel,
        out_shape=jax.ShapeDtypeStruct((t, d), BF16),
        grid=(ng, batch, nq),
        in_specs=[pl.BlockSpec((qb, w), lambda g, b, i: (b * nq + i, g)),
                  kv_spec(1, 2), kv_spec(1, 1), kv_spec(1, 0),
                  kv_spec(2, 2), kv_spec(2, 1), kv_spec(2, 0),
                  pl.BlockSpec((hb,) + gen.shape[1:], lambda g, b, i: (g, 0, 0))],
        out_specs=pl.BlockSpec((qb, w), lambda g, b, i: (b * nq + i, g)),
        scratch_shapes=[pltpu.VMEM((hb, qb, ATT_KB), F32)],
        compiler_params=_params("arbitrary", "arbitrary", "arbitrary"),
        name="attn_core",
    )(qkv, qkv, qkv, qkv, qkv, qkv, qkv, gen)


def kernel(x, norm_mix_g, norm_mlp_g, gla_w_in, gla_w_gate_up, gla_b_gate, gla_g_out, gla_w_out,
           att_w_in, att_g_q, att_g_k, att_rel_bias, att_w_out, mlp_w_up, mlp_w_down):
    batch, seq, d = x.shape
    depth = norm_mix_g.shape[0]
    xf = x.reshape(batch * seq, d)
    for i in range(depth):
        j = i // N_MIXERS
        g_mix = norm_mix_g[i].reshape(1, d)
        if i % N_MIXERS == 0:
            w_in = gla_w_in[j]
            n_main = w_in.shape[1] - GLA_GATE_RANK
            w_main = w_in[:, :n_main].astype(BF16)
            w_z = jnp.pad(w_in[:, n_main:], ((0, 0), (0, LANES - GLA_GATE_RANK))).astype(BF16)
            wg = jnp.pad(gla_w_gate_up[j], ((0, LANES - GLA_GATE_RANK), (0, 0)))
            proj, z = _norm_matmul(xf, g_mix, w_main, wz=w_z)
            mix = _gla_core(proj, z, wg, gla_b_gate[j].reshape(1, -1), gla_g_out[j].reshape(1, d),
                            batch, seq)
            w_out = gla_w_out[j]
        else:
            hd = att_g_q.shape[1]
            gh = jnp.concatenate([jnp.tile(att_g_q[j] * (hd ** -0.5 * LOG2E), ATT_HEADS),
                                  jnp.tile(att_g_k[j], ATT_HEADS),
                                  jnp.ones((d,), F32)]).reshape(1, -1)
            qkv = _norm_matmul(xf, g_mix, att_w_in[j].astype(BF16), gh=gh, hd=hd, n_norm=2 * d)
            mix = _attn_core(qkv, att_rel_bias[j], batch, seq)
            w_out = att_w_out[j]
        xf = _matmul_res(mix, w_out.astype(BF16), xf)
        xf = _mlp(xf, norm_mlp_g[i].reshape(1, d), mlp_w_up[i].astype(BF16),
                  mlp_w_down[i].astype(BF16))
    return xf.reshape(batch, seq, d)
```

```python
import functools

import jax
import jax.numpy as jnp
from jax import lax
from jax.experimental import pallas as pl
from jax.experimental.pallas import tpu as pltpu

F32 = jnp.float32
BF16 = jnp.bfloat16

EPS = 1e-6
CHUNK = 64
N_MIXERS = 2
GLA_HEADS = 4
GLA_GATE_RANK = 16
GLA_TAU = 16.0
ATT_HEADS = 16
LEFT_CHUNKS = 8
REL_CLIP = 256

LANES = 128
NORM_ROWS = 32
NORM_UNROLL = 4
VMEM_LIMIT = 56 * 1024 * 1024
NEG = -1e30
LOG2E = 1.4426950408889634

GLA_ROWS = 256
ATT_QB = 256
ATT_KB = ATT_QB + LEFT_CHUNKS * CHUNK
ATT_HB = 8


def _params(*sem):
    return pltpu.CompilerParams(dimension_semantics=sem, vmem_limit_bytes=VMEM_LIMIT)


def _rmsnorm_rows_to(x_ref, g_ref, dst_ref, copy_ref=None):
    rows = x_ref.shape[0]

    def body(r, carry):
        sl = pl.ds(pl.multiple_of(r * NORM_ROWS, NORM_ROWS), NORM_ROWS)
        x = x_ref[sl, :]
        ms = jnp.mean(x * x, axis=-1, keepdims=True)
        dst_ref[sl, :] = ((x * lax.rsqrt(ms + EPS)) * g_ref[...]).astype(dst_ref.dtype)
        if copy_ref is not None:
            copy_ref[sl, :] = x
        return carry

    lax.fori_loop(0, rows // NORM_ROWS, body, 0, unroll=NORM_UNROLL)


def _norm_matmul_headnorm_kernel(x_ref, g_ref, w_ref, gh_ref, o_ref, xn_ref, *, norm_blocks, hd):
    j = pl.program_id(1)

    @pl.when(j == 0)
    def _():
        _rmsnorm_rows_to(x_ref, g_ref, xn_ref)

    acc = jnp.dot(xn_ref[...], w_ref[...], preferred_element_type=F32)
    normed = j < norm_blocks
    for h in range(acc.shape[1] // hd):
        cs = slice(h * hd, (h + 1) * hd)
        a = acc[:, cs]
        ms = jnp.mean(a * a, axis=-1, keepdims=True)
        scale = jnp.where(normed, lax.rsqrt(ms + EPS), 1.0)
        o_ref[:, cs] = ((a * scale) * gh_ref[:, cs]).astype(o_ref.dtype)


def _norm_matmul_z_kernel(x_ref, g_ref, w_ref, wz_ref, o_ref, z_ref, xn_ref):
    @pl.when(pl.program_id(1) == 0)
    def _():
        _rmsnorm_rows_to(x_ref, g_ref, xn_ref)
        z_ref[...] = jnp.dot(xn_ref[...], wz_ref[...], preferred_element_type=F32)

    o_ref[...] = jnp.dot(xn_ref[...], w_ref[...], preferred_element_type=F32).astype(o_ref.dtype)


def _norm_matmul(x, g, w, wz=None, gh=None, hd=None, n_norm=0, *, tm=1024, tn=1024):
    t, d = x.shape
    n = w.shape[1]
    tm, tn = min(tm, t), min(tn, n)
    grid = (t // tm, n // tn)
    x_spec = pl.BlockSpec((tm, d), lambda i, j: (i, 0))
    g_spec = pl.BlockSpec((1, d), lambda i, j: (0, 0))
    w_spec = pl.BlockSpec((d, tn), lambda i, j: (0, j))
    o_spec = pl.BlockSpec((tm, tn), lambda i, j: (i, j))
    scratch = [pltpu.VMEM((tm, d), BF16)]
    if gh is not None:
        assert n_norm % tn == 0 and tn % hd == 0 and gh.shape[1] == n
        gh_spec = pl.BlockSpec((1, tn), lambda i, j: (0, j))
        return pl.pallas_call(
            functools.partial(_norm_matmul_headnorm_kernel, norm_blocks=n_norm // tn, hd=hd),
            out_shape=jax.ShapeDtypeStruct((t, n), BF16),
            grid=grid, in_specs=[x_spec, g_spec, w_spec, gh_spec], out_specs=o_spec,
            scratch_shapes=scratch, compiler_params=_params("parallel", "arbitrary"),
            name="norm_matmul_headnorm",
        )(x, g, w, gh)
    nz = wz.shape[1]
    wz_spec = pl.BlockSpec((d, nz), lambda i, j: (0, 0))
    z_spec = pl.BlockSpec((tm, nz), lambda i, j: (i, 0))
    return pl.pallas_call(
        _norm_matmul_z_kernel,
        out_shape=(jax.ShapeDtypeStruct((t, n), BF16), jax.ShapeDtypeStruct((t, nz), F32)),
        grid=grid, in_specs=[x_spec, g_spec, w_spec, wz_spec], out_specs=(o_spec, z_spec),
        scratch_shapes=scratch, compiler_params=_params("parallel", "arbitrary"),
        name="norm_matmul_z",
    )(x, g, w, wz)


def _matmul_res_kernel(a_ref, w_ref, r_ref, o_ref):
    o_ref[...] = r_ref[...] + jnp.dot(a_ref[...], w_ref[...], preferred_element_type=F32)


def _matmul_res(a, w, res, *, tm=1024, tn=1024):
    t, k = a.shape
    n = w.shape[1]
    tm, tn = min(tm, t), min(tn, n)
    return pl.pallas_call(
        _matmul_res_kernel,
        out_shape=jax.ShapeDtypeStruct((t, n), F32),
        grid=(t // tm, n // tn),
        in_specs=[pl.BlockSpec((tm, k), lambda i, j: (i, 0)),
                  pl.BlockSpec((k, tn), lambda i, j: (0, j)),
                  pl.BlockSpec((tm, tn), lambda i, j: (i, j))],
        out_specs=pl.BlockSpec((tm, tn), lambda i, j: (i, j)),
        compiler_params=_params("parallel", "parallel"),
        name="matmul_res",
    )(a, w, res)


def _mlp_kernel(x_ref, g_ref, wu_ref, wd_ref, o_ref, xn_ref):
    @pl.when(pl.program_id(1) == 0)
    def _():
        _rmsnorm_rows_to(x_ref, g_ref, xn_ref, copy_ref=o_ref)

    h = jnp.dot(xn_ref[...], wu_ref[...], preferred_element_type=F32)
    h = jnp.square(jnp.maximum(h, 0.0)).astype(BF16)
    o_ref[...] += jnp.dot(h, wd_ref[...], preferred_element_type=F32)


def _mlp(x, g, w_up, w_down, *, tm=512, tf=1024):
    t, d = x.shape
    f = w_up.shape[1]
    tm, tf = min(tm, t), min(tf, f)
    return pl.pallas_call(
        _mlp_kernel,
        out_shape=jax.ShapeDtypeStruct((t, d), F32),
        grid=(t // tm, f // tf),
        in_specs=[pl.BlockSpec((tm, d), lambda i, j: (i, 0)),
                  pl.BlockSpec((1, d), lambda i, j: (0, 0)),
                  pl.BlockSpec((d, tf), lambda i, j: (0, j)),
                  pl.BlockSpec((tf, d), lambda i, j: (j, 0))],
        out_specs=pl.BlockSpec((tm, d), lambda i, j: (i, 0)),
        scratch_shapes=[pltpu.VMEM((tm, d), BF16)],
        compiler_params=_params("parallel", "arbitrary"),
        name="mlp",
    )(x, g, w_up, w_down)


def _split_bf16(a):
    hi = a.astype(BF16)
    lo = (a - hi.astype(F32)).astype(BF16)
    return hi, lo


def _dot(a, b):
    return jnp.dot(a, b, preferred_element_type=F32)


def _dot_tn(a, b):
    return lax.dot_general(a, b, (((0,), (0,)), ((), ())), preferred_element_type=F32)


def _dot_nt(a, b):
    return lax.dot_general(a, b, (((1,), (1,)), ((), ())), preferred_element_type=F32)


def _gla_kernel(q_ref, k_ref, v_ref, r_ref, z_ref, wgh_ref, wgl_ref, bg_ref, go_ref, tri_ref,
                o_ref, state_ref):
    rows, dk = q_ref.shape
    nc = rows // CHUNK
    chunk = lambda c: slice(c * CHUNK, (c + 1) * CHUNK)

    @pl.when(pl.program_id(2) == 0)
    def _():
        state_ref[...] = jnp.zeros_like(state_ref)

    z_hi, z_lo = _split_bf16(z_ref[...])
    wg_hi = wgh_ref[...]
    logit = _dot(z_hi, wg_hi) + _dot(z_lo, wg_hi) + _dot(z_hi, wgl_ref[...]) + bg_ref[...]
    log_a = (jnp.minimum(logit, 0.0) - jnp.log1p(jnp.exp(-jnp.abs(logit)))) * (1.0 / GLA_TAU)

    la_hi, la_lo = _split_bf16(log_a)
    tri = tri_ref[...]
    b = _dot(tri, la_hi) + _dot(tri, la_lo)
    bl = [b[(c + 1) * CHUNK - 1:(c + 1) * CHUNK, :] for c in range(nc)]

    def dec(c1, c2):
        return jnp.exp(functools.reduce(jnp.add, bl[c1:c2]))

    q_dec = q_ref[...].astype(F32) * (dk ** -0.5) * jnp.exp(b)
    k = k_ref[...].astype(F32)
    k_inv = (k * jnp.exp(-b)).astype(BF16)
    k_end = [k[chunk(c)] * jnp.exp(bl[c] - b[chunk(c)]) for c in range(nc)]
    q_dec_bf = q_dec.astype(BF16)

    trow = lax.broadcasted_iota(jnp.int32, (CHUNK, rows), 0)
    scol = lax.broadcasted_iota(jnp.int32, (CHUNK, rows), 1)
    a_rows = []
    for c in range(nc):
        keys = [(k_end[cp] * dec(cp + 1, c) if cp + 1 < c else k_end[cp]).astype(BF16)
                for cp in range(c)]
        keys.append(k_inv[chunk(c)])
        if c + 1 < nc:
            keys.append(jnp.zeros(((nc - 1 - c) * CHUNK, dk), BF16))
        s = _dot_nt(q_dec_bf[chunk(c)], jnp.concatenate(keys, axis=0))
        a_rows.append(jnp.where(scol <= trow + c * CHUNK, s, 0.0).astype(BF16))
    a = jnp.concatenate(a_rows, axis=0)

    q_in = jnp.concatenate(
        [(q_dec[chunk(c)] * dec(0, c)).astype(BF16) if c else q_dec_bf[chunk(0)] for c in range(nc)],
        axis=0)
    k_out = jnp.concatenate(
        [(k_end[c] * dec(c + 1, nc) if c + 1 < nc else k_end[c]).astype(BF16) for c in range(nc)],
        axis=0)

    v = v_ref[...]
    state = state_ref[...]
    o = _dot(a, v) + _dot_nt(q_in, state.astype(BF16))
    state_ref[...] = state * dec(0, nc) + _dot_tn(v, k_out)

    for c in range(nc):
        oc = o[chunk(c)]
        ms = jnp.mean(oc * oc, axis=-1, keepdims=True)
        r = r_ref[chunk(c), :].astype(F32)
        y = (oc * lax.rsqrt(ms + EPS)) * go_ref[...]
        o_ref[chunk(c), :] = (y * (r * jax.nn.sigmoid(r))).astype(o_ref.dtype)


def _gla_core(proj, z, wg, bg, g_out, batch, seq):
    t = proj.shape[0]
    d = g_out.shape[1]
    dv = d // GLA_HEADS
    dk = dv // 2
    rows = min(GLA_ROWS, seq)
    nblk = seq // rows
    k_off = GLA_HEADS
    v_off = (2 * GLA_HEADS * dk) // dv
    r_off = v_off + GLA_HEADS
    wg_hi, wg_lo = _split_bf16(wg)
    idx = jnp.arange(rows)
    tri = ((idx[:, None] // CHUNK == idx[None, :] // CHUNK) & (idx[None, :] <= idx[:, None])).astype(BF16)
    rowmap = lambda b, h, n: b * nblk + n
    wg_spec = pl.BlockSpec((wg.shape[0], dk), lambda b, h, n: (0, h))
    return pl.pallas_call(
        _gla_kernel,
        out_shape=jax.ShapeDtypeStruct((t, d), BF16),
        grid=(batch, GLA_HEADS, nblk),
        in_specs=[pl.BlockSpec((rows, dk), lambda b, h, n: (rowmap(b, h, n), h)),
                  pl.BlockSpec((rows, dk), lambda b, h, n: (rowmap(b, h, n), k_off + h)),
                  pl.BlockSpec((rows, dv), lambda b, h, n: (rowmap(b, h, n), v_off + h)),
                  pl.BlockSpec((rows, dv), lambda b, h, n: (rowmap(b, h, n), r_off + h)),
                  pl.BlockSpec((rows, z.shape[1]), lambda b, h, n: (rowmap(b, h, n), 0)),
                  wg_spec, wg_spec,
                  pl.BlockSpec((1, dk), lambda b, h, n: (0, h)),
                  pl.BlockSpec((1, dv), lambda b, h, n: (0, h)),
                  pl.BlockSpec((rows, rows), lambda b, h, n: (0, 0))],
        out_specs=pl.BlockSpec((rows, dv), lambda b, h, n: (rowmap(b, h, n), h)),
        scratch_shapes=[pltpu.VMEM((dv, dk), F32)],
        compiler_params=_params("parallel", "parallel", "arbitrary"),
        name="gla_core",
    )(proj, proj, proj, proj, z, wg_hi, wg_lo, bg, g_out, tri)


def _attn_kernel(q_ref, k0_ref, k1_ref, k2_ref, v0_ref, v1_ref, v2_ref, gen_ref, o_ref, bias_ref):
    qb = q_ref.shape[0]
    hb = bias_ref.shape[0]
    hd = q_ref.shape[1] // hb
    nkb = ATT_KB // qb
    cpb = qb // CHUNK
    i = pl.program_id(2)
    k_refs = (k0_ref, k1_ref, k2_ref)
    v_refs = (v0_ref, v1_ref, v2_ref)

    @pl.when((pl.program_id(1) == 0) & (i == 0))
    def _():
        cq = lax.broadcasted_iota(jnp.int32, (qb, qb), 0) // CHUNK
        ck = lax.broadcasted_iota(jnp.int32, (qb, qb), 1) // CHUNK
        for j in range(nkb):
            band = (ck + cpb * j >= cq) & (ck + cpb * j <= cq + LEFT_CHUNKS)
            for h in range(hb):
                gen = jnp.broadcast_to(gen_ref[h, j:j + 1, :], (qb, 2 * qb))
                toe = pltpu.roll(gen, qb + 1, 1, stride=1, stride_axis=0)[:, :qb]
                bias_ref[h, :, j * qb:(j + 1) * qb] = jnp.where(band, toe, NEG)

    ones = jnp.ones((qb, hd), BF16)
    q_ext = jnp.full((qb, hd), 1.0 / hd, BF16)
    k_ext = jnp.concatenate(
        [jnp.full((qb, hd), jnp.where(i >= nkb - 1 - j, 0.0, NEG), F32).astype(BF16)
         for j in range(nkb)], axis=0)
    for h in range(hb):
        cs = slice(h * hd, (h + 1) * hd)
        q = jnp.concatenate([q_ref[:, cs], q_ext], axis=1)
        k = jnp.concatenate([jnp.concatenate([r[:, cs] for r in k_refs], axis=0), k_ext], axis=1)
        s = _dot_nt(q, k) + bias_ref[h]
        m = jnp.max(s, axis=-1, keepdims=True)
        p = jnp.exp2(s - m).astype(BF16)
        v1 = jnp.concatenate([jnp.concatenate([r[:, cs] for r in v_refs], axis=0),
                              jnp.concatenate([ones] * nkb, axis=0)], axis=1)
        acc = _dot(p, v1)
        o_ref[:, cs] = (acc[:, :hd] / acc[:, hd:]).astype(o_ref.dtype)


def _attn_bias_generators(rel_bias, qb):
    left = LEFT_CHUNKS * CHUNK
    x = jnp.arange(2 * qb)[None, :]
    j = jnp.arange((qb + left) // qb)[:, None]
    dist = left - qb * j + (qb - 1) - x
    idx = jnp.clip(dist, -(CHUNK - 1), REL_CLIP) + (CHUNK - 1)
    return rel_bias[:, idx].astype(F32) * LOG2E


def _attn_core(qkv, rel_bias, batch, seq):
    t = qkv.shape[0]
    d = qkv.shape[1] // 3
    hd = d // ATT_HEADS
    qb = ATT_QB
    assert seq % qb == 0 and ATT_KB % qb == 0 and ATT_KB // qb == 3
    nq = seq // qb
    hb = ATT_HB
    ng = ATT_HEADS // hb
    gen = _attn_bias_generators(rel_bias, qb)
    w = hb * hd

    def kv_spec(off, back):
        return pl.BlockSpec((qb, w), lambda g, b, i: (b * nq + jnp.maximum(i - back, 0), off * ng + g))

    return pl.pallas_call(
        _attn_kernel,
        out_shape=jax.ShapeDtypeStruct((t, d), BF16),
        grid=(ng, batch, nq),
        in_specs=[pl.BlockSpec((qb, w), lambda g, b, i: (b * nq + i, g)),
                  kv_spec(1, 2), kv_spec(1, 1), kv_spec(1, 0),
                  kv_spec(2, 2), kv_spec(2, 1), kv_spec(2, 0),
                  pl.BlockSpec((hb,) + gen.shape[1:], lambda g, b, i: (g, 0, 0))],
        out_specs=pl.BlockSpec((qb, w), lambda g, b, i: (b * nq + i, g)),
        scratch_shapes=[pltpu.VMEM((hb, qb, ATT_KB), F32)],
        compiler_params=_params("arbitrary", "arbitrary", "arbitrary"),
        name="attn_core",
    )(qkv, qkv, qkv, qkv, qkv, qkv, qkv, gen)


def kernel(x, norm_mix_g, norm_mlp_g, gla_w_in, gla_w_gate_up, gla_b_gate, gla_g_out, gla_w_out,
           att_w_in, att_g_q, att_g_k, att_rel_bias, att_w_out, mlp_w_up, mlp_w_down):
    batch, seq, d = x.shape
    depth = norm_mix_g.shape[0]
    xf = x.reshape(batch * seq, d)
    for i in range(depth):
        j = i // N_MIXERS
        g_mix = norm_mix_g[i].reshape(1, d)
        if i % N_MIXERS == 0:
            w_in = gla_w_in[j]
            n_main = w_in.shape[1] - GLA_GATE_RANK
            w_main = w_in[:, :n_main].astype(BF16)
            w_z = jnp.pad(w_in[:, n_main:], ((0, 0), (0, LANES - GLA_GATE_RANK))).astype(BF16)
            wg = jnp.pad(gla_w_gate_up[j], ((0, LANES - GLA_GATE_RANK), (0, 0)))
            proj, z = _norm_matmul(xf, g_mix, w_main, wz=w_z)
            mix = _gla_core(proj, z, wg, gla_b_gate[j].reshape(1, -1), gla_g_out[j].reshape(1, d),
                            batch, seq)
            w_out = gla_w_out[j]
        else:
            hd = att_g_q.shape[1]
            gh = jnp.concatenate([jnp.tile(att_g_q[j] * (hd ** -0.5 * LOG2E), ATT_HEADS),
                                  jnp.tile(att_g_k[j], ATT_HEADS),
                                  jnp.ones((d,), F32)]).reshape(1, -1)
            qkv = _norm_matmul(xf, g_mix, att_w_in[j].astype(BF16), gh=gh, hd=hd, n_norm=2 * d)
            mix = _attn_core(qkv, att_rel_bias[j], batch, seq)
            w_out = att_w_out[j]
        xf = _matmul_res(mix, w_out.astype(BF16), xf)
        xf = _mlp(xf, norm_mlp_g[i].reshape(1, d), mlp_w_up[i].astype(BF16),
                  mlp_w_down[i].astype(BF16))
    return xf.reshape(batch, seq, d)
```

```python
import functools

import jax
import jax.numpy as jnp
from jax import lax
from jax.experimental import pallas as pl
from jax.experimental.pallas import tpu as pltpu

F32 = jnp.float32
BF16 = jnp.bfloat16

EPS = 1e-6
CHUNK = 64
N_MIXERS = 2
GLA_HEADS = 4
GLA_GATE_RANK = 16
GLA_TAU = 16.0
ATT_HEADS = 16
LEFT_CHUNKS = 8
REL_CLIP = 256

LANES = 128
NORM_ROWS = 32
NORM_UNROLL = 4
VMEM_LIMIT = 56 * 1024 * 1024
NEG = -1e30
LOG2E = 1.4426950408889634

GLA_ROWS = 256
ATT_QB = 256
ATT_KB = ATT_QB + LEFT_CHUNKS * CHUNK
ATT_HB = 8


def _params(*sem):
    return pltpu.CompilerParams(dimension_semantics=sem, vmem_limit_bytes=VMEM_LIMIT)


def _rmsnorm_rows_to(x_ref, g_ref, dst_ref, copy_ref=None):
    rows = x_ref.shape[0]

    def body(r, carry):
        sl = pl.ds(pl.multiple_of(r * NORM_ROWS, NORM_ROWS), NORM_ROWS)
        x = x_ref[sl, :]
        ms = jnp.mean(x * x, axis=-1, keepdims=True)
        dst_ref[sl, :] = ((x * lax.rsqrt(ms + EPS)) * g_ref[...]).astype(dst_ref.dtype)
        if copy_ref is not None:
            copy_ref[sl, :] = x
        return carry

    lax.fori_loop(0, rows // NORM_ROWS, body, 0, unroll=NORM_UNROLL)


def _norm_matmul_headnorm_kernel(x_ref, g_ref, w_ref, gh_ref, o_ref, xn_ref, *, norm_blocks, hd):
    j = pl.program_id(1)

    @pl.when(j == 0)
    def _():
        _rmsnorm_rows_to(x_ref, g_ref, xn_ref)

    acc = jnp.dot(xn_ref[...], w_ref[...], preferred_element_type=F32)
    normed = j < norm_blocks
    for h in range(acc.shape[1] // hd):
        cs = slice(h * hd, (h + 1) * hd)
        a = acc[:, cs]
        ms = jnp.mean(a * a, axis=-1, keepdims=True)
        scale = jnp.where(normed, lax.rsqrt(ms + EPS), 1.0)
        o_ref[:, cs] = ((a * scale) * gh_ref[:, cs]).astype(o_ref.dtype)


def _norm_matmul_z_kernel(x_ref, g_ref, w_ref, wz_ref, o_ref, z_ref, xn_ref):
    @pl.when(pl.program_id(1) == 0)
    def _():
        _rmsnorm_rows_to(x_ref, g_ref, xn_ref)
        z_ref[...] = jnp.dot(xn_ref[...], wz_ref[...], preferred_element_type=F32)

    o_ref[...] = jnp.dot(xn_ref[...], w_ref[...], preferred_element_type=F32).astype(o_ref.dtype)


def _norm_matmul(x, g, w, layer, n, wz=None, gh=None, hd=None, n_norm=0, *, tm=1024, tn=1024):
    t, d = x.shape
    tm, tn = min(tm, t), min(tn, n)
    grid = (t // tm, n // tn)
    x_spec = pl.BlockSpec((tm, d), lambda i, j: (i, 0))
    g_spec = pl.BlockSpec((1, d), lambda i, j: (0, 0))
    w_spec = pl.BlockSpec((None, d, tn), lambda i, j: (layer, 0, j))
    o_spec = pl.BlockSpec((tm, tn), lambda i, j: (i, j))
    scratch = [pltpu.VMEM((tm, d), BF16)]
    if gh is not None:
        assert n_norm % tn == 0 and tn % hd == 0 and gh.shape[1] == n
        gh_spec = pl.BlockSpec((1, tn), lambda i, j: (0, j))
        return pl.pallas_call(
            functools.partial(_norm_matmul_headnorm_kernel, norm_blocks=n_norm // tn, hd=hd),
            out_shape=jax.ShapeDtypeStruct((t, n), BF16),
            grid=grid, in_specs=[x_spec, g_spec, w_spec, gh_spec], out_specs=o_spec,
            scratch_shapes=scratch, compiler_params=_params("parallel", "arbitrary"),
            name="norm_matmul_headnorm",
        )(x, g, w, gh)
    nz = wz.shape[1]
    wz_spec = pl.BlockSpec((d, nz), lambda i, j: (0, 0))
    z_spec = pl.BlockSpec((tm, nz), lambda i, j: (i, 0))
    return pl.pallas_call(
        _norm_matmul_z_kernel,
        out_shape=(jax.ShapeDtypeStruct((t, n), BF16), jax.ShapeDtypeStruct((t, nz), F32)),
        grid=grid, in_specs=[x_spec, g_spec, w_spec, wz_spec], out_specs=(o_spec, z_spec),
        scratch_shapes=scratch, compiler_params=_params("parallel", "arbitrary"),
        name="norm_matmul_z",
    )(x, g, w, wz)


def _matmul_res_kernel(a_ref, w_ref, r_ref, o_ref):
    o_ref[...] = r_ref[...] + jnp.dot(a_ref[...], w_ref[...], preferred_element_type=F32)


def _matmul_res(a, w, layer, res, *, tm=1024, tn=1024):
    t, k = a.shape
    n = w.shape[2]
    tm, tn = min(tm, t), min(tn, n)
    return pl.pallas_call(
        _matmul_res_kernel,
        out_shape=jax.ShapeDtypeStruct((t, n), F32),
        grid=(t // tm, n // tn),
        in_specs=[pl.BlockSpec((tm, k), lambda i, j: (i, 0)),
                  pl.BlockSpec((None, k, tn), lambda i, j: (layer, 0, j)),
                  pl.BlockSpec((tm, tn), lambda i, j: (i, j))],
        out_specs=pl.BlockSpec((tm, tn), lambda i, j: (i, j)),
        compiler_params=_params("parallel", "parallel"),
        name="matmul_res",
    )(a, w, res)


def _mlp_kernel(x_ref, g_ref, wu_ref, wd_ref, o_ref, xn_ref):
    @pl.when(pl.program_id(1) == 0)
    def _():
        _rmsnorm_rows_to(x_ref, g_ref, xn_ref, copy_ref=o_ref)

    h = jnp.dot(xn_ref[...], wu_ref[...], preferred_element_type=F32)
    h = jnp.square(jnp.maximum(h, 0.0)).astype(BF16)
    o_ref[...] += jnp.dot(h, wd_ref[...], preferred_element_type=F32)


def _mlp(x, g, w_up, w_down, layer, *, tm=512, tf=1024):
    t, d = x.shape
    f = w_up.shape[2]
    tm, tf = min(tm, t), min(tf, f)
    return pl.pallas_call(
        _mlp_kernel,
        out_shape=jax.ShapeDtypeStruct((t, d), F32),
        grid=(t // tm, f // tf),
        in_specs=[pl.BlockSpec((tm, d), lambda i, j: (i, 0)),
                  pl.BlockSpec((1, d), lambda i, j: (0, 0)),
                  pl.BlockSpec((None, d, tf), lambda i, j: (layer, 0, j)),
                  pl.BlockSpec((None, tf, d), lambda i, j: (layer, j, 0))],
        out_specs=pl.BlockSpec((tm, d), lambda i, j: (i, 0)),
        scratch_shapes=[pltpu.VMEM((tm, d), BF16)],
        compiler_params=_params("parallel", "arbitrary"),
        name="mlp",
    )(x, g, w_up, w_down)


def _split_bf16(a):
    hi = a.astype(BF16)
    lo = (a - hi.astype(F32)).astype(BF16)
    return hi, lo


def _dot(a, b):
    return jnp.dot(a, b, preferred_element_type=F32)


def _dot_tn(a, b):
    return lax.dot_general(a, b, (((0,), (0,)), ((), ())), preferred_element_type=F32)


def _dot_nt(a, b):
    return lax.dot_general(a, b, (((1,), (1,)), ((), ())), preferred_element_type=F32)


def _gla_kernel(q_ref, k_ref, v_ref, r_ref, z_ref, wgh_ref, wgl_ref, bg_ref, go_ref, tri_ref,
                o_ref, state_ref):
    rows, dk = q_ref.shape
    nc = rows // CHUNK
    chunk = lambda c: slice(c * CHUNK, (c + 1) * CHUNK)

    @pl.when(pl.program_id(2) == 0)
    def _():
        state_ref[...] = jnp.zeros_like(state_ref)

    z_hi, z_lo = _split_bf16(z_ref[...])
    wg_hi = wgh_ref[...]
    logit = _dot(z_hi, wg_hi) + _dot(z_lo, wg_hi) + _dot(z_hi, wgl_ref[...]) + bg_ref[...]
    log_a = (jnp.minimum(logit, 0.0) - jnp.log1p(jnp.exp(-jnp.abs(logit)))) * (1.0 / GLA_TAU)

    la_hi, la_lo = _split_bf16(log_a)
    tri = tri_ref[...]
    b = _dot(tri, la_hi) + _dot(tri, la_lo)
    bl = [b[(c + 1) * CHUNK - 1:(c + 1) * CHUNK, :] for c in range(nc)]

    def dec(c1, c2):
        return jnp.exp(functools.reduce(jnp.add, bl[c1:c2]))

    q_dec = q_ref[...].astype(F32) * (dk ** -0.5) * jnp.exp(b)
    k = k_ref[...].astype(F32)
    k_inv = (k * jnp.exp(-b)).astype(BF16)
    k_end = [k[chunk(c)] * jnp.exp(bl[c] - b[chunk(c)]) for c in range(nc)]
    q_dec_bf = q_dec.astype(BF16)

    trow = lax.broadcasted_iota(jnp.int32, (CHUNK, rows), 0)
    scol = lax.broadcasted_iota(jnp.int32, (CHUNK, rows), 1)
    a_rows = []
    for c in range(nc):
        keys = [(k_end[cp] * dec(cp + 1, c) if cp + 1 < c else k_end[cp]).astype(BF16)
                for cp in range(c)]
        keys.append(k_inv[chunk(c)])
        if c + 1 < nc:
            keys.append(jnp.zeros(((nc - 1 - c) * CHUNK, dk), BF16))
        s = _dot_nt(q_dec_bf[chunk(c)], jnp.concatenate(keys, axis=0))
        a_rows.append(jnp.where(scol <= trow + c * CHUNK, s, 0.0).astype(BF16))
    a = jnp.concatenate(a_rows, axis=0)

    q_in = jnp.concatenate(
        [(q_dec[chunk(c)] * dec(0, c)).astype(BF16) if c else q_dec_bf[chunk(0)] for c in range(nc)],
        axis=0)
    k_out = jnp.concatenate(
        [(k_end[c] * dec(c + 1, nc) if c + 1 < nc else k_end[c]).astype(BF16) for c in range(nc)],
        axis=0)

    v = v_ref[...]
    state = state_ref[...]
    o = _dot(a, v) + _dot_nt(q_in, state.astype(BF16))
    state_ref[...] = state * dec(0, nc) + _dot_tn(v, k_out)

    for c in range(nc):
        oc = o[chunk(c)]
        ms = jnp.mean(oc * oc, axis=-1, keepdims=True)
        r = r_ref[chunk(c), :].astype(F32)
        y = (oc * lax.rsqrt(ms + EPS)) * go_ref[...]
        o_ref[chunk(c), :] = (y * (r * jax.nn.sigmoid(r))).astype(o_ref.dtype)


def _gla_core(proj, z, wg, bg, g_out, batch, seq):
    t = proj.shape[0]
    d = g_out.shape[1]
    dv = d // GLA_HEADS
    dk = dv // 2
    rows = min(GLA_ROWS, seq)
    nblk = seq // rows
    k_off = GLA_HEADS
    v_off = (2 * GLA_HEADS * dk) // dv
    r_off = v_off + GLA_HEADS
    wg_hi, wg_lo = _split_bf16(wg)
    idx = jnp.arange(rows)
    tri = ((idx[:, None] // CHUNK == idx[None, :] // CHUNK) & (idx[None, :] <= idx[:, None])).astype(BF16)
    rowmap = lambda b, h, n: b * nblk + n
    wg_spec = pl.BlockSpec((wg.shape[0], dk), lambda b, h, n: (0, h))
    return pl.pallas_call(
        _gla_kernel,
        out_shape=jax.ShapeDtypeStruct((t, d), BF16),
        grid=(batch, GLA_HEADS, nblk),
        in_specs=[pl.BlockSpec((rows, dk), lambda b, h, n: (rowmap(b, h, n), h)),
                  pl.BlockSpec((rows, dk), lambda b, h, n: (rowmap(b, h, n), k_off + h)),
                  pl.BlockSpec((rows, dv), lambda b, h, n: (rowmap(b, h, n), v_off + h)),
                  pl.BlockSpec((rows, dv), lambda b, h, n: (rowmap(b, h, n), r_off + h)),
                  pl.BlockSpec((rows, z.shape[1]), lambda b, h, n: (rowmap(b, h, n), 0)),
                  wg_spec, wg_spec,
                  pl.BlockSpec((1, dk), lambda b, h, n: (0, h)),
                  pl.BlockSpec((1, dv), lambda b, h, n: (0, h)),
                  pl.BlockSpec((rows, rows), lambda b, h, n: (0, 0))],
        out_specs=pl.BlockSpec((rows, dv), lambda b, h, n: (rowmap(b, h, n), h)),
        scratch_shapes=[pltpu.VMEM((dv, dk), F32)],
        compiler_params=_params("parallel", "parallel", "arbitrary"),
        name="gla_core",
    )(proj, proj, proj, proj, z, wg_hi, wg_lo, bg, g_out, tri)


def _attn_kernel(q_ref, k0_ref, k1_ref, k2_ref, v0_ref, v1_ref, v2_ref, gen_ref, o_ref, bias_ref):
    qb = q_ref.shape[0]
    hb = bias_ref.shape[0]
    hd = q_ref.shape[1] // hb
    nkb = ATT_KB // qb
    cpb = qb // CHUNK
    i = pl.program_id(2)
    k_refs = (k0_ref, k1_ref, k2_ref)
    v_refs = (v0_ref, v1_ref, v2_ref)

    @pl.when((pl.program_id(1) == 0) & (i == 0))
    def _():
        cq = lax.broadcasted_iota(jnp.int32, (qb, qb), 0) // CHUNK
        ck = lax.broadcasted_iota(jnp.int32, (qb, qb), 1) // CHUNK
        for j in range(nkb):
            band = (ck + cpb * j >= cq) & (ck + cpb * j <= cq + LEFT_CHUNKS)
            for h in range(hb):
                gen = jnp.broadcast_to(gen_ref[h, j:j + 1, :], (qb, 2 * qb))
                toe = pltpu.roll(gen, qb + 1, 1, stride=1, stride_axis=0)[:, :qb]
                bias_ref[h, :, j * qb:(j + 1) * qb] = jnp.where(band, toe, NEG)

    ones = jnp.ones((qb, hd), BF16)
    q_ext = jnp.full((qb, hd), 1.0 / hd, BF16)
    k_ext = jnp.concatenate(
        [jnp.full((qb, hd), jnp.where(i >= nkb - 1 - j, 0.0, NEG), F32).astype(BF16)
         for j in range(nkb)], axis=0)
    for h in range(hb):
        cs = slice(h * hd, (h + 1) * hd)
        q = jnp.concatenate([q_ref[:, cs], q_ext], axis=1)
        k = jnp.concatenate([jnp.concatenate([r[:, cs] for r in k_refs], axis=0), k_ext], axis=1)
        s = _dot_nt(q, k) + bias_ref[h]
        m = jnp.max(s, axis=-1, keepdims=True)
        p = jnp.exp2(s - m).astype(BF16)
        v1 = jnp.concatenate([jnp.concatenate([r[:, cs] for r in v_refs], axis=0),
                              jnp.concatenate([ones] * nkb, axis=0)], axis=1)
        acc = _dot(p, v1)
        o_ref[:, cs] = (acc[:, :hd] / acc[:, hd:]).astype(o_ref.dtype)


def _attn_bias_generators(rel_bias, qb):
    left = LEFT_CHUNKS * CHUNK
    x = jnp.arange(2 * qb)[None, :]
    j = jnp.arange((qb + left) // qb)[:, None]
    dist = left - qb * j + (qb - 1) - x
    idx = jnp.clip(dist, -(CHUNK - 1), REL_CLIP) + (CHUNK - 1)
    return rel_bias[:, idx].astype(F32) * LOG2E


def _attn_core(qkv, rel_bias, batch, seq):
    t = qkv.shape[0]
    d = qkv.shape[1] // 3
    hd = d // ATT_HEADS
    qb = ATT_QB
    assert seq % qb == 0 and ATT_KB % qb == 0 and ATT_KB // qb == 3
    nq = seq // qb
    hb = ATT_HB
    ng = ATT_HEADS // hb
    gen = _attn_bias_generators(rel_bias, qb)
    w = hb * hd

    def kv_spec(off, back):
        return pl.BlockSpec((qb, w), lambda g, b, i: (b * nq + jnp.maximum(i - back, 0), off * ng + g))

    return pl.pallas_call(
        _attn_kernel,
        out_shape=jax.ShapeDtypeStruct((t, d), BF16),
        grid=(ng, batch, nq),
        in_specs=[pl.BlockSpec((qb, w), lambda g, b, i: (b * nq + i, g)),
                  kv_spec(1, 2), kv_spec(1, 1), kv_spec(1, 0),
                  kv_spec(2, 2), kv_spec(2, 1), kv_spec(2, 0),
                  pl.BlockSpec((hb,) + gen.shape[1:], lambda g, b, i: (g, 0, 0))],
        out_specs=pl.BlockSpec((qb, w), lambda g, b, i: (b * nq + i, g)),
        scratch_shapes=[pltpu.VMEM((hb, qb, ATT_KB), F32)],
        compiler_params=_params("arbitrary", "arbitrary", "arbitrary"),
        name="attn_core",
    )(qkv, qkv, qkv, qkv, qkv, qkv, qkv, gen)


def kernel(x, norm_mix_g, norm_mlp_g, gla_w_in, gla_w_gate_up, gla_b_gate, gla_g_out, gla_w_out,
           att_w_in, att_g_q, att_g_k, att_rel_bias, att_w_out, mlp_w_up, mlp_w_down):
    batch, seq, d = x.shape
    depth = norm_mix_g.shape[0]
    xf = x.reshape(batch * seq, d)
    gla_w_in_bf, gla_w_out_bf = gla_w_in.astype(BF16), gla_w_out.astype(BF16)
    att_w_in_bf, att_w_out_bf = att_w_in.astype(BF16), att_w_out.astype(BF16)
    mlp_w_up_bf, mlp_w_down_bf = mlp_w_up.astype(BF16), mlp_w_down.astype(BF16)
    for i in range(depth):
        j = i // N_MIXERS
        g_mix = norm_mix_g[i].reshape(1, d)
        if i % N_MIXERS == 0:
            n_main = gla_w_in.shape[2] - GLA_GATE_RANK
            w_z = jnp.pad(gla_w_in[j, :, n_main:], ((0, 0), (0, LANES - GLA_GATE_RANK))).astype(BF16)
            wg = jnp.pad(gla_w_gate_up[j], ((0, LANES - GLA_GATE_RANK), (0, 0)))
            proj, z = _norm_matmul(xf, g_mix, gla_w_in_bf, j, n_main, wz=w_z)
            mix = _gla_core(proj, z, wg, gla_b_gate[j].reshape(1, -1), gla_g_out[j].reshape(1, d),
                            batch, seq)
            w_out = gla_w_out_bf
        else:
            hd = att_g_q.shape[1]
            gh = jnp.concatenate([jnp.tile(att_g_q[j] * (hd ** -0.5 * LOG2E), ATT_HEADS),
                                  jnp.tile(att_g_k[j], ATT_HEADS),
                                  jnp.ones((d,), F32)]).reshape(1, -1)
            qkv = _norm_matmul(xf, g_mix, att_w_in_bf, j, 3 * d, gh=gh, hd=hd, n_norm=2 * d)
            mix = _attn_core(qkv, att_rel_bias[j], batch, seq)
            w_out = att_w_out_bf
        xf = _matmul_res(mix, w_out, j, xf)
        xf = _mlp(xf, norm_mlp_g[i].reshape(1, d), mlp_w_up_bf, mlp_w_down_bf, i)
    return xf.reshape(batch, seq, d)
```

```python
import functools

import jax
import jax.numpy as jnp
from jax import lax
from jax.experimental import pallas as pl
from jax.experimental.pallas import tpu as pltpu

F32 = jnp.float32
BF16 = jnp.bfloat16

EPS = 1e-6
CHUNK = 64
N_MIXERS = 2
GLA_HEADS = 4
GLA_GATE_RANK = 16
GLA_TAU = 16.0
ATT_HEADS = 16
LEFT_CHUNKS = 8
REL_CLIP = 256

LANES = 128
NORM_ROWS = 32
NORM_UNROLL = 4
VMEM_LIMIT = 56 * 1024 * 1024
NEG = -1e30
LOG2E = 1.4426950408889634

GLA_ROWS = 256
GLA_HB = 2
ATT_QB = 256
ATT_KB = ATT_QB + LEFT_CHUNKS * CHUNK
ATT_HB = 8


def _params(*sem):
    return pltpu.CompilerParams(dimension_semantics=sem, vmem_limit_bytes=VMEM_LIMIT)


def _rmsnorm_rows_to(x_ref, g_ref, dst_ref, copy_ref=None):
    rows = x_ref.shape[0]

    def body(r, carry):
        sl = pl.ds(pl.multiple_of(r * NORM_ROWS, NORM_ROWS), NORM_ROWS)
        x = x_ref[sl, :]
        ms = jnp.mean(x * x, axis=-1, keepdims=True)
        dst_ref[sl, :] = ((x * lax.rsqrt(ms + EPS)) * g_ref[...]).astype(dst_ref.dtype)
        if copy_ref is not None:
            copy_ref[sl, :] = x
        return carry

    lax.fori_loop(0, rows // NORM_ROWS, body, 0, unroll=NORM_UNROLL)


def _norm_matmul_headnorm_kernel(x_ref, g_ref, w_ref, gh_ref, o_ref, xn_ref, *, norm_blocks, hd):
    j = pl.program_id(1)

    @pl.when(j == 0)
    def _():
        _rmsnorm_rows_to(x_ref, g_ref, xn_ref)

    acc = jnp.dot(xn_ref[...], w_ref[...], preferred_element_type=F32)
    normed = j < norm_blocks
    for h in range(acc.shape[1] // hd):
        cs = slice(h * hd, (h + 1) * hd)
        a = acc[:, cs]
        ms = jnp.mean(a * a, axis=-1, keepdims=True)
        scale = jnp.where(normed, lax.rsqrt(ms + EPS), 1.0)
        o_ref[:, cs] = ((a * scale) * gh_ref[:, cs]).astype(o_ref.dtype)


def _norm_matmul_z_kernel(x_ref, g_ref, w_ref, wz_ref, o_ref, z_ref, xn_ref):
    @pl.when(pl.program_id(1) == 0)
    def _():
        _rmsnorm_rows_to(x_ref, g_ref, xn_ref)
        z_ref[...] = jnp.dot(xn_ref[...], wz_ref[...], preferred_element_type=F32)

    o_ref[...] = jnp.dot(xn_ref[...], w_ref[...], preferred_element_type=F32).astype(o_ref.dtype)


def _norm_matmul(x, g, w, layer, n, wz=None, gh=None, hd=None, n_norm=0, *, tm=1024, tn=1024):
    t, d = x.shape
    tm, tn = min(tm, t), min(tn, n)
    grid = (t // tm, n // tn)
    x_spec = pl.BlockSpec((tm, d), lambda i, j: (i, 0))
    g_spec = pl.BlockSpec((1, d), lambda i, j: (0, 0))
    w_spec = pl.BlockSpec((None, d, tn), lambda i, j: (layer, 0, j))
    o_spec = pl.BlockSpec((tm, tn), lambda i, j: (i, j))
    scratch = [pltpu.VMEM((tm, d), BF16)]
    if gh is not None:
        assert n_norm % tn == 0 and tn % hd == 0 and gh.shape[1] == n
        gh_spec = pl.BlockSpec((1, tn), lambda i, j: (0, j))
        return pl.pallas_call(
            functools.partial(_norm_matmul_headnorm_kernel, norm_blocks=n_norm // tn, hd=hd),
            out_shape=jax.ShapeDtypeStruct((t, n), BF16),
            grid=grid, in_specs=[x_spec, g_spec, w_spec, gh_spec], out_specs=o_spec,
            scratch_shapes=scratch, compiler_params=_params("parallel", "arbitrary"),
            name="norm_matmul_headnorm",
        )(x, g, w, gh)
    nz = wz.shape[1]
    wz_spec = pl.BlockSpec((d, nz), lambda i, j: (0, 0))
    z_spec = pl.BlockSpec((tm, nz), lambda i, j: (i, 0))
    return pl.pallas_call(
        _norm_matmul_z_kernel,
        out_shape=(jax.ShapeDtypeStruct((t, n), BF16), jax.ShapeDtypeStruct((t, nz), F32)),
        grid=grid, in_specs=[x_spec, g_spec, w_spec, wz_spec], out_specs=(o_spec, z_spec),
        scratch_shapes=scratch, compiler_params=_params("parallel", "arbitrary"),
        name="norm_matmul_z",
    )(x, g, w, wz)


def _matmul_res_kernel(a_ref, w_ref, r_ref, o_ref):
    o_ref[...] = r_ref[...] + jnp.dot(a_ref[...], w_ref[...], preferred_element_type=F32)


def _matmul_res(a, w, layer, res, *, tm=1024, tn=1024):
    t, k = a.shape
    n = w.shape[2]
    tm, tn = min(tm, t), min(tn, n)
    return pl.pallas_call(
        _matmul_res_kernel,
        out_shape=jax.ShapeDtypeStruct((t, n), F32),
        grid=(t // tm, n // tn),
        in_specs=[pl.BlockSpec((tm, k), lambda i, j: (i, 0)),
                  pl.BlockSpec((None, k, tn), lambda i, j: (layer, 0, j)),
                  pl.BlockSpec((tm, tn), lambda i, j: (i, j))],
        out_specs=pl.BlockSpec((tm, tn), lambda i, j: (i, j)),
        compiler_params=_params("parallel", "parallel"),
        name="matmul_res",
    )(a, w, res)


def _mlp_kernel(x_ref, g_ref, wu_ref, wd_ref, o_ref, xn_ref):
    @pl.when(pl.program_id(1) == 0)
    def _():
        _rmsnorm_rows_to(x_ref, g_ref, xn_ref, copy_ref=o_ref)

    h = jnp.dot(xn_ref[...], wu_ref[...], preferred_element_type=F32)
    h = jnp.square(jnp.maximum(h, 0.0)).astype(BF16)
    o_ref[...] += jnp.dot(h, wd_ref[...], preferred_element_type=F32)


def _mlp(x, g, w_up, w_down, layer, *, tm=512, tf=1024):
    t, d = x.shape
    f = w_up.shape[2]
    tm, tf = min(tm, t), min(tf, f)
    return pl.pallas_call(
        _mlp_kernel,
        out_shape=jax.ShapeDtypeStruct((t, d), F32),
        grid=(t // tm, f // tf),
        in_specs=[pl.BlockSpec((tm, d), lambda i, j: (i, 0)),
                  pl.BlockSpec((1, d), lambda i, j: (0, 0)),
                  pl.BlockSpec((None, d, tf), lambda i, j: (layer, 0, j)),
                  pl.BlockSpec((None, tf, d), lambda i, j: (layer, j, 0))],
        out_specs=pl.BlockSpec((tm, d), lambda i, j: (i, 0)),
        scratch_shapes=[pltpu.VMEM((tm, d), BF16)],
        compiler_params=_params("parallel", "arbitrary"),
        name="mlp",
    )(x, g, w_up, w_down)


def _split_bf16(a):
    hi = a.astype(BF16)
    lo = (a - hi.astype(F32)).astype(BF16)
    return hi, lo


def _dot(a, b):
    return jnp.dot(a, b, preferred_element_type=F32)


def _dot_tn(a, b):
    return lax.dot_general(a, b, (((0,), (0,)), ((), ())), preferred_element_type=F32)


def _dot_nt(a, b):
    return lax.dot_general(a, b, (((1,), (1,)), ((), ())), preferred_element_type=F32)


def _gla_kernel(q_ref, k_ref, v_ref, r_ref, z_ref, wgh_ref, wgl_ref, bg_ref, go_ref, tri_ref,
                o_ref, state_ref):
    hb = state_ref.shape[0]

    @pl.when(pl.program_id(2) == 0)
    def _():
        state_ref[...] = jnp.zeros_like(state_ref)

    z_parts = _split_bf16(z_ref[...])
    tri = tri_ref[...]
    for h in range(hb):
        _gla_head(h, q_ref, k_ref, v_ref, r_ref, z_parts, wgh_ref, wgl_ref, bg_ref, go_ref, tri,
                  o_ref, state_ref)


def _gla_head(h, q_ref, k_ref, v_ref, r_ref, z_parts, wgh_ref, wgl_ref, bg_ref, go_ref, tri,
              o_ref, state_ref):
    rows = q_ref.shape[0]
    hb = state_ref.shape[0]
    dk = q_ref.shape[1] // hb
    dv = v_ref.shape[1] // hb
    ks = slice(h * dk, (h + 1) * dk)
    vs = slice(h * dv, (h + 1) * dv)
    nc = rows // CHUNK
    chunk = lambda c: slice(c * CHUNK, (c + 1) * CHUNK)

    z_hi, z_lo = z_parts
    wg_hi = wgh_ref[:, ks]
    logit = _dot(z_hi, wg_hi) + _dot(z_lo, wg_hi) + _dot(z_hi, wgl_ref[:, ks]) + bg_ref[:, ks]
    t = logit * LOG2E
    log_a = (jnp.minimum(t, 0.0) - jnp.log2(1.0 + jnp.exp2(-jnp.abs(t)))) * (1.0 / GLA_TAU)

    la_hi, la_lo = _split_bf16(log_a)
    b = _dot(tri, la_hi) + _dot(tri, la_lo)
    bl = [b[(c + 1) * CHUNK - 1:(c + 1) * CHUNK, :] for c in range(nc)]

    def dec(c1, c2):
        return jnp.exp2(functools.reduce(jnp.add, bl[c1:c2]))

    q_dec = q_ref[:, ks].astype(F32) * (dk ** -0.5) * jnp.exp2(b)
    k = k_ref[:, ks].astype(F32)
    k_inv = (k * jnp.exp2(-b)).astype(BF16)
    k_end = [k[chunk(c)] * jnp.exp2(bl[c] - b[chunk(c)]) for c in range(nc)]
    q_dec_bf = q_dec.astype(BF16)

    trow = lax.broadcasted_iota(jnp.int32, (CHUNK, rows), 0)
    scol = lax.broadcasted_iota(jnp.int32, (CHUNK, rows), 1)
    a_rows = []
    for c in range(nc):
        keys = [(k_end[cp] * dec(cp + 1, c) if cp + 1 < c else k_end[cp]).astype(BF16)
                for cp in range(c)]
        keys.append(k_inv[chunk(c)])
        if c + 1 < nc:
            keys.append(jnp.zeros(((nc - 1 - c) * CHUNK, dk), BF16))
        s = _dot_nt(q_dec_bf[chunk(c)], jnp.concatenate(keys, axis=0))
        a_rows.append(jnp.where(scol <= trow + c * CHUNK, s, 0.0).astype(BF16))
    a = jnp.concatenate(a_rows, axis=0)

    q_in = jnp.concatenate(
        [(q_dec[chunk(c)] * dec(0, c)).astype(BF16) if c else q_dec_bf[chunk(0)] for c in range(nc)],
        axis=0)
    k_out = jnp.concatenate(
        [(k_end[c] * dec(c + 1, nc) if c + 1 < nc else k_end[c]).astype(BF16) for c in range(nc)],
        axis=0)

    v = v_ref[:, vs]
    state = state_ref[h]
    o = _dot(a, v) + _dot_nt(q_in, state.astype(BF16))
    state_ref[h] = state * dec(0, nc) + _dot_tn(v, k_out)

    for c in range(nc):
        oc = o[chunk(c)]
        ms = jnp.mean(oc * oc, axis=-1, keepdims=True)
        r = r_ref[chunk(c), vs].astype(F32)
        y = (oc * lax.rsqrt(ms + EPS)) * go_ref[:, vs]
        o_ref[chunk(c), vs] = (y * (r * (jnp.tanh(0.5 * r) + 1.0))).astype(o_ref.dtype)


def _gla_core(proj, z, wg, bg, g_half, batch, seq):
    t = proj.shape[0]
    d = g_half.shape[1]
    dv = d // GLA_HEADS
    dk = dv // 2
    rows = min(GLA_ROWS, seq)
    nblk = seq // rows
    hb = GLA_HB
    ng = GLA_HEADS // hb
    wk, wv = hb * dk, hb * dv
    k_off = ng
    v_off = (2 * GLA_HEADS * dk) // wv
    r_off = v_off + ng
    assert (2 * GLA_HEADS * dk) % wv == 0
    wg_hi, wg_lo = _split_bf16(wg)
    idx = jnp.arange(rows)
    tri = ((idx[:, None] // CHUNK == idx[None, :] // CHUNK) & (idx[None, :] <= idx[:, None])).astype(BF16)
    rowmap = lambda b, h, n: b * nblk + n
    wg_spec = pl.BlockSpec((wg.shape[0], wk), lambda b, h, n: (0, h))
    return pl.pallas_call(
        _gla_kernel,
        out_shape=jax.ShapeDtypeStruct((t, d), BF16),
        grid=(batch, ng, nblk),
        in_specs=[pl.BlockSpec((rows, wk), lambda b, h, n: (rowmap(b, h, n), h)),
                  pl.BlockSpec((rows, wk), lambda b, h, n: (rowmap(b, h, n), k_off + h)),
                  pl.BlockSpec((rows, wv), lambda b, h, n: (rowmap(b, h, n), v_off + h)),
                  pl.BlockSpec((rows, wv), lambda b, h, n: (rowmap(b, h, n), r_off + h)),
                  pl.BlockSpec((rows, z.shape[1]), lambda b, h, n: (rowmap(b, h, n), 0)),
                  wg_spec, wg_spec,
                  pl.BlockSpec((1, wk), lambda b, h, n: (0, h)),
                  pl.BlockSpec((1, wv), lambda b, h, n: (0, h)),
                  pl.BlockSpec((rows, rows), lambda b, h, n: (0, 0))],
        out_specs=pl.BlockSpec((rows, wv), lambda b, h, n: (rowmap(b, h, n), h)),
        scratch_shapes=[pltpu.VMEM((hb, dv, dk), F32)],
        compiler_params=_params("parallel", "parallel", "arbitrary"),
        name="gla_core",
    )(proj, proj, proj, proj, z, wg_hi, wg_lo, bg, g_half, tri)


def _attn_kernel(q_ref, k0_ref, k1_ref, k2_ref, v0_ref, v1_ref, v2_ref, gen_ref, o_ref, bias_ref):
    qb = q_ref.shape[0]
    hb = bias_ref.shape[0]
    hd = q_ref.shape[1] // hb
    nkb = ATT_KB // qb
    cpb = qb // CHUNK
    i = pl.program_id(2)
    k_refs = (k0_ref, k1_ref, k2_ref)
    v_refs = (v0_ref, v1_ref, v2_ref)

    @pl.when((pl.program_id(1) == 0) & (i == 0))
    def _():
        cq = lax.broadcasted_iota(jnp.int32, (qb, qb), 0) // CHUNK
        ck = lax.broadcasted_iota(jnp.int32, (qb, qb), 1) // CHUNK
        for j in range(nkb):
            band = (ck + cpb * j >= cq) & (ck + cpb * j <= cq + LEFT_CHUNKS)
            for h in range(hb):
                gen = jnp.broadcast_to(gen_ref[h, j:j + 1, :], (qb, 2 * qb))
                toe = pltpu.roll(gen, qb + 1, 1, stride=1, stride_axis=0)[:, :qb]
                bias_ref[h, :, j * qb:(j + 1) * qb] = jnp.where(band, toe, NEG)

    ones = jnp.ones((qb, hd), BF16)
    q_ext = jnp.full((qb, hd), 1.0 / hd, BF16)
    k_ext = jnp.concatenate(
        [jnp.full((qb, hd), jnp.where(i >= nkb - 1 - j, 0.0, NEG), F32).astype(BF16)
         for j in range(nkb)], axis=0)
    for h in range(hb):
        cs = slice(h * hd, (h + 1) * hd)
        q = jnp.concatenate([q_ref[:, cs], q_ext], axis=1)
        k = jnp.concatenate([jnp.concatenate([r[:, cs] for r in k_refs], axis=0), k_ext], axis=1)
        s = _dot_nt(q, k) + bias_ref[h]
        m = jnp.max(s, axis=-1, keepdims=True)
        p = jnp.exp2(s - m).astype(BF16)
        v1 = jnp.concatenate([jnp.concatenate([r[:, cs] for r in v_refs], axis=0),
                              jnp.concatenate([ones] * nkb, axis=0)], axis=1)
        acc = _dot(p, v1)
        o_ref[:, cs] = (acc[:, :hd] / acc[:, hd:]).astype(o_ref.dtype)


def _attn_bias_generators(rel_bias, qb):
    left = LEFT_CHUNKS * CHUNK
    x = jnp.arange(2 * qb)[None, :]
    j = jnp.arange((qb + left) // qb)[:, None]
    dist = left - qb * j + (qb - 1) - x
    idx = jnp.clip(dist, -(CHUNK - 1), REL_CLIP) + (CHUNK - 1)
    return rel_bias[:, idx].astype(F32) * LOG2E


def _attn_core(qkv, rel_bias, batch, seq):
    t = qkv.shape[0]
    d = qkv.shape[1] // 3
    hd = d // ATT_HEADS
    qb = ATT_QB
    assert seq % qb == 0 and ATT_KB % qb == 0 and ATT_KB // qb == 3
    nq = seq // qb
    hb = ATT_HB
    ng = ATT_HEADS // hb
    gen = _attn_bias_generators(rel_bias, qb)
    w = hb * hd

    def kv_spec(off, back):
        return pl.BlockSpec((qb, w), lambda g, b, i: (b * nq + jnp.maximum(i - back, 0), off * ng + g))

    return pl.pallas_call(
        _attn_kernel,
        out_shape=jax.ShapeDtypeStruct((t, d), BF16),
        grid=(ng, batch, nq),
        in_specs=[pl.BlockSpec((qb, w), lambda g, b, i: (b * nq + i, g)),
                  kv_spec(1, 2), kv_spec(1, 1), kv_spec(1, 0),
                  kv_spec(2, 2), kv_spec(2, 1), kv_spec(2, 0),
                  pl.BlockSpec((hb,) + gen.shape[1:], lambda g, b, i: (g, 0, 0))],
        out_specs=pl.BlockSpec((qb, w), lambda g, b, i: (b * nq + i, g)),
        scratch_shapes=[pltpu.VMEM((hb, qb, ATT_KB), F32)],
        compiler_params=_params("arbitrary", "arbitrary", "arbitrary"),
        name="attn_core",
    )(qkv, qkv, qkv, qkv, qkv, qkv, qkv, gen)


def kernel(x, norm_mix_g, norm_mlp_g, gla_w_in, gla_w_gate_up, gla_b_gate, gla_g_out, gla_w_out,
           att_w_in, att_g_q, att_g_k, att_rel_bias, att_w_out, mlp_w_up, mlp_w_down):
    batch, seq, d = x.shape
    depth = norm_mix_g.shape[0]
    xf = x.reshape(batch * seq, d)
    gla_w_in_bf, gla_w_out_bf = gla_w_in.astype(BF16), gla_w_out.astype(BF16)
    att_w_in_bf, att_w_out_bf = att_w_in.astype(BF16), att_w_out.astype(BF16)
    mlp_w_up_bf, mlp_w_down_bf = mlp_w_up.astype(BF16), mlp_w_down.astype(BF16)
    for i in range(depth):
        j = i // N_MIXERS
        g_mix = norm_mix_g[i].reshape(1, d)
        if i % N_MIXERS == 0:
            n_main = gla_w_in.shape[2] - GLA_GATE_RANK
            w_z = jnp.pad(gla_w_in[j, :, n_main:], ((0, 0), (0, LANES - GLA_GATE_RANK))).astype(BF16)
            wg = jnp.pad(gla_w_gate_up[j], ((0, LANES - GLA_GATE_RANK), (0, 0)))
            proj, z = _norm_matmul(xf, g_mix, gla_w_in_bf, j, n_main, wz=w_z)
            mix = _gla_core(proj, z, wg, gla_b_gate[j].reshape(1, -1),
                            0.5 * gla_g_out[j].reshape(1, d), batch, seq)
            w_out = gla_w_out_bf
        else:
            hd = att_g_q.shape[1]
            gh = jnp.concatenate([jnp.tile(att_g_q[j] * (hd ** -0.5 * LOG2E), ATT_HEADS),
                                  jnp.tile(att_g_k[j], ATT_HEADS),
                                  jnp.ones((d,), F32)]).reshape(1, -1)
            qkv = _norm_matmul(xf, g_mix, att_w_in_bf, j, 3 * d, gh=gh, hd=hd, n_norm=2 * d)
            mix = _attn_core(qkv, att_rel_bias[j], batch, seq)
            w_out = att_w_out_bf
        xf = _matmul_res(mix, w_out, j, xf)
        xf = _mlp(xf, norm_mlp_g[i].reshape(1, d), mlp_w_up_bf, mlp_w_down_bf, i)
    return xf.reshape(batch, seq, d)
```

```python
import functools

import jax
import jax.numpy as jnp
from jax import lax
from jax.experimental import pallas as pl
from jax.experimental.pallas import tpu as pltpu

F32 = jnp.float32
BF16 = jnp.bfloat16

EPS = 1e-6
CHUNK = 64
N_MIXERS = 2
GLA_HEADS = 4
GLA_GATE_RANK = 16
GLA_TAU = 16.0
ATT_HEADS = 16
LEFT_CHUNKS = 8
REL_CLIP = 256

LANES = 128
NORM_ROWS = 32
NORM_UNROLL = 4
VMEM_LIMIT = 56 * 1024 * 1024
NEG = -1e30
LOG2E = 1.4426950408889634

GLA_ROWS = 256
GLA_HB = 4
ATT_QB = 256
ATT_KB = ATT_QB + LEFT_CHUNKS * CHUNK
ATT_HB = 8
ATT_INTERLEAVE = 1


def _params(*sem):
    return pltpu.CompilerParams(dimension_semantics=sem, vmem_limit_bytes=VMEM_LIMIT)


def _interleave(chains, width):
    for start in range(0, len(chains), width):
        group = chains[start:start + width]
        while group:
            alive = []
            for chain in group:
                try:
                    next(chain)
                    alive.append(chain)
                except StopIteration:
                    pass
            group = alive


def _rmsnorm_rows_to(x_ref, g_ref, dst_ref, copy_ref=None):
    rows = x_ref.shape[0]

    def body(r, carry):
        sl = pl.ds(pl.multiple_of(r * NORM_ROWS, NORM_ROWS), NORM_ROWS)
        x = x_ref[sl, :]
        ms = jnp.mean(x * x, axis=-1, keepdims=True)
        dst_ref[sl, :] = ((x * lax.rsqrt(ms + EPS)) * g_ref[...]).astype(dst_ref.dtype)
        if copy_ref is not None:
            copy_ref[sl, :] = x
        return carry

    lax.fori_loop(0, rows // NORM_ROWS, body, 0, unroll=NORM_UNROLL)


def _norm_matmul_headnorm_kernel(x_ref, g_ref, w_ref, gh_ref, o_ref, xn_ref, *, norm_blocks, hd):
    j = pl.program_id(1)

    @pl.when(j == 0)
    def _():
        _rmsnorm_rows_to(x_ref, g_ref, xn_ref)

    acc = jnp.dot(xn_ref[...], w_ref[...], preferred_element_type=F32)
    normed = j < norm_blocks
    for h in range(acc.shape[1] // hd):
        cs = slice(h * hd, (h + 1) * hd)
        a = acc[:, cs]
        ms = jnp.mean(a * a, axis=-1, keepdims=True)
        scale = jnp.where(normed, lax.rsqrt(ms + EPS), 1.0)
        o_ref[:, cs] = ((a * scale) * gh_ref[:, cs]).astype(o_ref.dtype)


def _norm_matmul_z_kernel(x_ref, g_ref, w_ref, wz_ref, o_ref, z_ref, xn_ref):
    @pl.when(pl.program_id(1) == 0)
    def _():
        _rmsnorm_rows_to(x_ref, g_ref, xn_ref)
        z_ref[...] = jnp.dot(xn_ref[...], wz_ref[...], preferred_element_type=F32)

    o_ref[...] = jnp.dot(xn_ref[...], w_ref[...], preferred_element_type=F32).astype(o_ref.dtype)


def _norm_matmul(x, g, w, layer, n, wz=None, gh=None, hd=None, n_norm=0, *, tm=1024, tn=1024):
    t, d = x.shape
    tm, tn = min(tm, t), min(tn, n)
    grid = (t // tm, n // tn)
    x_spec = pl.BlockSpec((tm, d), lambda i, j: (i, 0))
    g_spec = pl.BlockSpec((1, d), lambda i, j: (0, 0))
    w_spec = pl.BlockSpec((None, d, tn), lambda i, j: (layer, 0, j))
    o_spec = pl.BlockSpec((tm, tn), lambda i, j: (i, j))
    scratch = [pltpu.VMEM((tm, d), BF16)]
    if gh is not None:
        assert n_norm % tn == 0 and tn % hd == 0 and gh.shape[1] == n
        gh_spec = pl.BlockSpec((1, tn), lambda i, j: (0, j))
        return pl.pallas_call(
            functools.partial(_norm_matmul_headnorm_kernel, norm_blocks=n_norm // tn, hd=hd),
            out_shape=jax.ShapeDtypeStruct((t, n), BF16),
            grid=grid, in_specs=[x_spec, g_spec, w_spec, gh_spec], out_specs=o_spec,
            scratch_shapes=scratch, compiler_params=_params("parallel", "arbitrary"),
            name="norm_matmul_headnorm",
        )(x, g, w, gh)
    nz = wz.shape[1]
    wz_spec = pl.BlockSpec((d, nz), lambda i, j: (0, 0))
    z_spec = pl.BlockSpec((tm, nz), lambda i, j: (i, 0))
    return pl.pallas_call(
        _norm_matmul_z_kernel,
        out_shape=(jax.ShapeDtypeStruct((t, n), BF16), jax.ShapeDtypeStruct((t, nz), F32)),
        grid=grid, in_specs=[x_spec, g_spec, w_spec, wz_spec], out_specs=(o_spec, z_spec),
        scratch_shapes=scratch, compiler_params=_params("parallel", "arbitrary"),
        name="norm_matmul_z",
    )(x, g, w, wz)


def _matmul_res_kernel(a_ref, w_ref, r_ref, o_ref):
    o_ref[...] = r_ref[...] + jnp.dot(a_ref[...], w_ref[...], preferred_element_type=F32)


def _matmul_res(a, w, layer, res, *, tm=1024, tn=1024):
    t, k = a.shape
    n = w.shape[2]
    tm, tn = min(tm, t), min(tn, n)
    return pl.pallas_call(
        _matmul_res_kernel,
        out_shape=jax.ShapeDtypeStruct((t, n), F32),
        grid=(t // tm, n // tn),
        in_specs=[pl.BlockSpec((tm, k), lambda i, j: (i, 0)),
                  pl.BlockSpec((None, k, tn), lambda i, j: (layer, 0, j)),
                  pl.BlockSpec((tm, tn), lambda i, j: (i, j))],
        out_specs=pl.BlockSpec((tm, tn), lambda i, j: (i, j)),
        compiler_params=_params("parallel", "parallel"),
        name="matmul_res",
    )(a, w, res)


def _mlp_kernel(x_ref, g_ref, wu_ref, wd_ref, o_ref, xn_ref):
    @pl.when(pl.program_id(1) == 0)
    def _():
        _rmsnorm_rows_to(x_ref, g_ref, xn_ref, copy_ref=o_ref)

    h = jnp.dot(xn_ref[...], wu_ref[...], preferred_element_type=F32)
    h = jnp.square(jnp.maximum(h, 0.0)).astype(BF16)
    o_ref[...] += jnp.dot(h, wd_ref[...], preferred_element_type=F32)


def _mlp(x, g, w_up, w_down, layer, *, tm=512, tf=1024):
    t, d = x.shape
    f = w_up.shape[2]
    tm, tf = min(tm, t), min(tf, f)
    return pl.pallas_call(
        _mlp_kernel,
        out_shape=jax.ShapeDtypeStruct((t, d), F32),
        grid=(t // tm, f // tf),
        in_specs=[pl.BlockSpec((tm, d), lambda i, j: (i, 0)),
                  pl.BlockSpec((1, d), lambda i, j: (0, 0)),
                  pl.BlockSpec((None, d, tf), lambda i, j: (layer, 0, j)),
                  pl.BlockSpec((None, tf, d), lambda i, j: (layer, j, 0))],
        out_specs=pl.BlockSpec((tm, d), lambda i, j: (i, 0)),
        scratch_shapes=[pltpu.VMEM((tm, d), BF16)],
        compiler_params=_params("parallel", "arbitrary"),
        name="mlp",
    )(x, g, w_up, w_down)


def _split_bf16(a):
    hi = a.astype(BF16)
    lo = (a - hi.astype(F32)).astype(BF16)
    return hi, lo


def _dot(a, b):
    return jnp.dot(a, b, preferred_element_type=F32)


def _dot_tn(a, b):
    return lax.dot_general(a, b, (((0,), (0,)), ((), ())), preferred_element_type=F32)


def _dot_nt(a, b):
    return lax.dot_general(a, b, (((1,), (1,)), ((), ())), preferred_element_type=F32)


def _gla_kernel(q_ref, k_ref, v_ref, r_ref, z_ref, wgh_ref, wgl_ref, bg_ref, go_ref, tri_ref,
                o_ref, state_ref):
    hb = state_ref.shape[0]

    @pl.when(pl.program_id(2) == 0)
    def _():
        state_ref[...] = jnp.zeros_like(state_ref)

    z_parts = _split_bf16(z_ref[...])
    tri = tri_ref[...]
    _interleave([_gla_head(h, q_ref, k_ref, v_ref, r_ref, z_parts, wgh_ref, wgl_ref, bg_ref, go_ref,
                           tri, o_ref, state_ref) for h in range(hb)], hb)


def _gla_head(h, q_ref, k_ref, v_ref, r_ref, z_parts, wgh_ref, wgl_ref, bg_ref, go_ref, tri,
              o_ref, state_ref):
    rows = q_ref.shape[0]
    hb = state_ref.shape[0]
    dk = q_ref.shape[1] // hb
    dv = v_ref.shape[1] // hb
    ks = slice(h * dk, (h + 1) * dk)
    vs = slice(h * dv, (h + 1) * dv)
    nc = rows // CHUNK
    chunk = lambda c: slice(c * CHUNK, (c + 1) * CHUNK)

    z_hi, z_lo = z_parts
    wg_hi = wgh_ref[:, ks]
    logit = _dot(z_hi, wg_hi) + _dot(z_lo, wg_hi) + _dot(z_hi, wgl_ref[:, ks]) + bg_ref[:, ks]
    t = logit * LOG2E
    yield
    log_a = (jnp.minimum(t, 0.0) - jnp.log2(1.0 + jnp.exp2(-jnp.abs(t)))) * (1.0 / GLA_TAU)

    la_hi, la_lo = _split_bf16(log_a)
    b = _dot(tri, la_hi) + _dot(tri, la_lo)
    yield
    bl = [b[(c + 1) * CHUNK - 1:(c + 1) * CHUNK, :] for c in range(nc)]

    def dec(c1, c2):
        return jnp.exp2(functools.reduce(jnp.add, bl[c1:c2]))

    q_dec = q_ref[:, ks].astype(F32) * (dk ** -0.5) * jnp.exp2(b)
    k = k_ref[:, ks].astype(F32)
    k_inv = (k * jnp.exp2(-b)).astype(BF16)
    k_end = [k[chunk(c)] * jnp.exp2(bl[c] - b[chunk(c)]) for c in range(nc)]
    q_dec_bf = q_dec.astype(BF16)
    yield

    trow = lax.broadcasted_iota(jnp.int32, (CHUNK, rows), 0)
    scol = lax.broadcasted_iota(jnp.int32, (CHUNK, rows), 1)
    a_rows = []
    for c in range(nc):
        keys = [(k_end[cp] * dec(cp + 1, c) if cp + 1 < c else k_end[cp]).astype(BF16)
                for cp in range(c)]
        keys.append(k_inv[chunk(c)])
        if c + 1 < nc:
            keys.append(jnp.zeros(((nc - 1 - c) * CHUNK, dk), BF16))
        s = _dot_nt(q_dec_bf[chunk(c)], jnp.concatenate(keys, axis=0))
        a_rows.append(jnp.where(scol <= trow + c * CHUNK, s, 0.0).astype(BF16))
    a = jnp.concatenate(a_rows, axis=0)
    yield

    q_in = jnp.concatenate(
        [(q_dec[chunk(c)] * dec(0, c)).astype(BF16) if c else q_dec_bf[chunk(0)] for c in range(nc)],
        axis=0)
    k_out = jnp.concatenate(
        [(k_end[c] * dec(c + 1, nc) if c + 1 < nc else k_end[c]).astype(BF16) for c in range(nc)],
        axis=0)

    v = v_ref[:, vs]
    state = state_ref[h]
    o = _dot(a, v) + _dot_nt(q_in, state.astype(BF16))
    state_ref[h] = state * dec(0, nc) + _dot_tn(v, k_out)
    yield

    for c in range(nc):
        oc = o[chunk(c)]
        ms = jnp.mean(oc * oc, axis=-1, keepdims=True)
        r = r_ref[chunk(c), vs].astype(F32)
        y = (oc * lax.rsqrt(ms + EPS)) * go_ref[:, vs]
        o_ref[chunk(c), vs] = (y * (r * (jnp.tanh(0.5 * r) + 1.0))).astype(o_ref.dtype)


def _gla_core(proj, z, wg, bg, g_half, batch, seq):
    t = proj.shape[0]
    d = g_half.shape[1]
    dv = d // GLA_HEADS
    dk = dv // 2
    rows = min(GLA_ROWS, seq)
    nblk = seq // rows
    hb = GLA_HB
    ng = GLA_HEADS // hb
    wk, wv = hb * dk, hb * dv
    k_off = ng
    v_off = (2 * GLA_HEADS * dk) // wv
    r_off = v_off + ng
    assert (2 * GLA_HEADS * dk) % wv == 0
    wg_hi, wg_lo = _split_bf16(wg)
    idx = jnp.arange(rows)
    tri = ((idx[:, None] // CHUNK == idx[None, :] // CHUNK) & (idx[None, :] <= idx[:, None])).astype(BF16)
    rowmap = lambda b, h, n: b * nblk + n
    wg_spec = pl.BlockSpec((wg.shape[0], wk), lambda b, h, n: (0, h))
    return pl.pallas_call(
        _gla_kernel,
        out_shape=jax.ShapeDtypeStruct((t, d), BF16),
        grid=(batch, ng, nblk),
        in_specs=[pl.BlockSpec((rows, wk), lambda b, h, n: (rowmap(b, h, n), h)),
                  pl.BlockSpec((rows, wk), lambda b, h, n: (rowmap(b, h, n), k_off + h)),
                  pl.BlockSpec((rows, wv), lambda b, h, n: (rowmap(b, h, n), v_off + h)),
                  pl.BlockSpec((rows, wv), lambda b, h, n: (rowmap(b, h, n), r_off + h)),
                  pl.BlockSpec((rows, z.shape[1]), lambda b, h, n: (rowmap(b, h, n), 0)),
                  wg_spec, wg_spec,
                  pl.BlockSpec((1, wk), lambda b, h, n: (0, h)),
                  pl.BlockSpec((1, wv), lambda b, h, n: (0, h)),
                  pl.BlockSpec((rows, rows), lambda b, h, n: (0, 0))],
        out_specs=pl.BlockSpec((rows, wv), lambda b, h, n: (rowmap(b, h, n), h)),
        scratch_shapes=[pltpu.VMEM((hb, dv, dk), F32)],
        compiler_params=_params("parallel", "parallel", "arbitrary"),
        name="gla_core",
    )(proj, proj, proj, proj, z, wg_hi, wg_lo, bg, g_half, tri)


def _attn_kernel(q_ref, k0_ref, k1_ref, k2_ref, v0_ref, v1_ref, v2_ref, gen_ref, o_ref, bias_ref):
    qb = q_ref.shape[0]
    hb = bias_ref.shape[0]
    hd = q_ref.shape[1] // hb
    nkb = ATT_KB // qb
    cpb = qb // CHUNK
    i = pl.program_id(2)
    k_refs = (k0_ref, k1_ref, k2_ref)
    v_refs = (v0_ref, v1_ref, v2_ref)

    @pl.when((pl.program_id(1) == 0) & (i == 0))
    def _():
        cq = lax.broadcasted_iota(jnp.int32, (qb, qb), 0) // CHUNK
        ck = lax.broadcasted_iota(jnp.int32, (qb, qb), 1) // CHUNK
        for j in range(nkb):
            band = (ck + cpb * j >= cq) & (ck + cpb * j <= cq + LEFT_CHUNKS)
            for h in range(hb):
                gen = jnp.broadcast_to(gen_ref[h, j:j + 1, :], (qb, 2 * qb))
                toe = pltpu.roll(gen, qb + 1, 1, stride=1, stride_axis=0)[:, :qb]
                bias_ref[h, :, j * qb:(j + 1) * qb] = jnp.where(band, toe, NEG)

    ones = jnp.ones((qb, hd), BF16)
    q_ext = jnp.full((qb, hd), 1.0 / hd, BF16)
    k_ext = jnp.concatenate(
        [jnp.full((qb, hd), jnp.where(i >= nkb - 1 - j, 0.0, NEG), F32).astype(BF16)
         for j in range(nkb)], axis=0)
    def head(h):
        cs = slice(h * hd, (h + 1) * hd)
        q = jnp.concatenate([q_ref[:, cs], q_ext], axis=1)
        k = jnp.concatenate([jnp.concatenate([r[:, cs] for r in k_refs], axis=0), k_ext], axis=1)
        s = _dot_nt(q, k) + bias_ref[h]
        yield
        m = jnp.max(s, axis=-1, keepdims=True)
        yield
        p = jnp.exp2(s - m).astype(BF16)
        yield
        v1 = jnp.concatenate([jnp.concatenate([r[:, cs] for r in v_refs], axis=0),
                              jnp.concatenate([ones] * nkb, axis=0)], axis=1)
        acc = _dot(p, v1)
        yield
        o_ref[:, cs] = (acc[:, :hd] / acc[:, hd:]).astype(o_ref.dtype)

    _interleave([head(h) for h in range(hb)], ATT_INTERLEAVE)


def _attn_bias_generators(rel_bias, qb):
    left = LEFT_CHUNKS * CHUNK
    x = jnp.arange(2 * qb)[None, :]
    j = jnp.arange((qb + left) // qb)[:, None]
    dist = left - qb * j + (qb - 1) - x
    idx = jnp.clip(dist, -(CHUNK - 1), REL_CLIP) + (CHUNK - 1)
    return rel_bias[:, idx].astype(F32) * LOG2E


def _attn_core(qkv, rel_bias, batch, seq):
    t = qkv.shape[0]
    d = qkv.shape[1] // 3
    hd = d // ATT_HEADS
    qb = ATT_QB
    assert seq % qb == 0 and ATT_KB % qb == 0 and ATT_KB // qb == 3
    nq = seq // qb
    hb = ATT_HB
    ng = ATT_HEADS // hb
    gen = _attn_bias_generators(rel_bias, qb)
    w = hb * hd

    def kv_spec(off, back):
        return pl.BlockSpec((qb, w), lambda g, b, i: (b * nq + jnp.maximum(i - back, 0), off * ng + g))

    return pl.pallas_call(
        _attn_kernel,
        out_shape=jax.ShapeDtypeStruct((t, d), BF16),
        grid=(ng, batch, nq),
        in_specs=[pl.BlockSpec((qb, w), lambda g, b, i: (b * nq + i, g)),
                  kv_spec(1, 2), kv_spec(1, 1), kv_spec(1, 0),
                  kv_spec(2, 2), kv_spec(2, 1), kv_spec(2, 0),
                  pl.BlockSpec((hb,) + gen.shape[1:], lambda g, b, i: (g, 0, 0))],
        out_specs=pl.BlockSpec((qb, w), lambda g, b, i: (b * nq + i, g)),
        scratch_shapes=[pltpu.VMEM((hb, qb, ATT_KB), F32)],
        compiler_params=_params("arbitrary", "arbitrary", "arbitrary"),
        name="attn_core",
    )(qkv, qkv, qkv, qkv, qkv, qkv, qkv, gen)


def kernel(x, norm_mix_g, norm_mlp_g, gla_w_in, gla_w_gate_up, gla_b_gate, gla_g_out, gla_w_out,
           att_w_in, att_g_q, att_g_k, att_rel_bias, att_w_out, mlp_w_up, mlp_w_down):
    batch, seq, d = x.shape
    depth = norm_mix_g.shape[0]
    xf = x.reshape(batch * seq, d)
    gla_w_in_bf, gla_w_out_bf = gla_w_in.astype(BF16), gla_w_out.astype(BF16)
    att_w_in_bf, att_w_out_bf = att_w_in.astype(BF16), att_w_out.astype(BF16)
    mlp_w_up_bf, mlp_w_down_bf = mlp_w_up.astype(BF16), mlp_w_down.astype(BF16)
    for i in range(depth):
        j = i // N_MIXERS
        g_mix = norm_mix_g[i].reshape(1, d)
        if i % N_MIXERS == 0:
            n_main = gla_w_in.shape[2] - GLA_GATE_RANK
            w_z = jnp.pad(gla_w_in[j, :, n_main:], ((0, 0), (0, LANES - GLA_GATE_RANK))).astype(BF16)
            wg = jnp.pad(gla_w_gate_up[j], ((0, LANES - GLA_GATE_RANK), (0, 0)))
            proj, z = _norm_matmul(xf, g_mix, gla_w_in_bf, j, n_main, wz=w_z)
            mix = _gla_core(proj, z, wg, gla_b_gate[j].reshape(1, -1),
                            0.5 * gla_g_out[j].reshape(1, d), batch, seq)
            w_out = gla_w_out_bf
        else:
            hd = att_g_q.shape[1]
            gh = jnp.concatenate([jnp.tile(att_g_q[j] * (hd ** -0.5 * LOG2E), ATT_HEADS),
                                  jnp.tile(att_g_k[j], ATT_HEADS),
                                  jnp.ones((d,), F32)]).reshape(1, -1)
            qkv = _norm_matmul(xf, g_mix, att_w_in_bf, j, 3 * d, gh=gh, hd=hd, n_norm=2 * d)
            mix = _attn_core(qkv, att_rel_bias[j], batch, seq)
            w_out = att_w_out_bf
        xf = _matmul_res(mix, w_out, j, xf)
        xf = _mlp(xf, norm_mlp_g[i].reshape(1, d), mlp_w_up_bf, mlp_w_down_bf, i)
    return xf.reshape(batch, seq, d)
```

```python
import functools

import jax
import jax.numpy as jnp
from jax import lax
from jax.experimental import pallas as pl
from jax.experimental.pallas import tpu as pltpu

F32 = jnp.float32
BF16 = jnp.bfloat16

EPS = 1e-6
CHUNK = 64
N_MIXERS = 2
GLA_HEADS = 4
GLA_GATE_RANK = 16
GLA_TAU = 16.0
ATT_HEADS = 16
LEFT_CHUNKS = 8
REL_CLIP = 256

LANES = 128
NORM_ROWS = 32
NORM_UNROLL = 4
HEADNORM_SPLIT = 2
VMEM_LIMIT = 56 * 1024 * 1024
NEG = -1e30
LOG2E = 1.4426950408889634

GLA_ROWS = 256
GLA_HB = 4
ATT_QB = 256
ATT_KB = ATT_QB + LEFT_CHUNKS * CHUNK
ATT_HB = 16
ATT_INTERLEAVE = 1


def _params(*sem):
    return pltpu.CompilerParams(dimension_semantics=sem, vmem_limit_bytes=VMEM_LIMIT)


def _interleave(chains, width):
    for start in range(0, len(chains), width):
        group = chains[start:start + width]
        while group:
            alive = []
            for chain in group:
                try:
                    next(chain)
                    alive.append(chain)
                except StopIteration:
                    pass
            group = alive


def _rmsnorm_rows_to(x_ref, g_ref, dst_ref, copy_ref=None):
    rows = x_ref.shape[0]

    def body(r, carry):
        sl = pl.ds(pl.multiple_of(r * NORM_ROWS, NORM_ROWS), NORM_ROWS)
        x = x_ref[sl, :]
        ms = jnp.mean(x * x, axis=-1, keepdims=True)
        dst_ref[sl, :] = ((x * lax.rsqrt(ms + EPS)) * g_ref[...]).astype(dst_ref.dtype)
        if copy_ref is not None:
            copy_ref[sl, :] = x
        return carry

    lax.fori_loop(0, rows // NORM_ROWS, body, 0, unroll=NORM_UNROLL)


def _norm_matmul_headnorm_kernel(x_ref, g_ref, w_ref, gh_ref, o_ref, xn_ref, *, norm_blocks, hd):
    j = pl.program_id(1)

    @pl.when(j == 0)
    def _():
        _rmsnorm_rows_to(x_ref, g_ref, xn_ref)

    normed = j < norm_blocks
    rows = xn_ref.shape[0] // HEADNORM_SPLIT
    for part in range(HEADNORM_SPLIT):
        rs = slice(part * rows, (part + 1) * rows)
        acc = jnp.dot(xn_ref[rs, :], w_ref[...], preferred_element_type=F32)
        for h in range(acc.shape[1] // hd):
            cs = slice(h * hd, (h + 1) * hd)
            a = acc[:, cs]
            ms = jnp.mean(a * a, axis=-1, keepdims=True)
            scale = jnp.where(normed, lax.rsqrt(ms + EPS), 1.0)
            o_ref[rs, cs] = ((a * scale) * gh_ref[:, cs]).astype(o_ref.dtype)


def _norm_matmul_z_kernel(x_ref, g_ref, w_ref, wz_ref, o_ref, z_ref, xn_ref):
    @pl.when(pl.program_id(1) == 0)
    def _():
        _rmsnorm_rows_to(x_ref, g_ref, xn_ref)
        z_ref[...] = jnp.dot(xn_ref[...], wz_ref[...], preferred_element_type=F32)

    o_ref[...] = jnp.dot(xn_ref[...], w_ref[...], preferred_element_type=F32).astype(o_ref.dtype)


def _norm_matmul(x, g, w, layer, n, wz=None, gh=None, hd=None, n_norm=0, *, tm=1024, tn=1024):
    t, d = x.shape
    tm, tn = min(tm, t), min(tn, n)
    grid = (t // tm, n // tn)
    x_spec = pl.BlockSpec((tm, d), lambda i, j: (i, 0))
    g_spec = pl.BlockSpec((1, d), lambda i, j: (0, 0))
    w_spec = pl.BlockSpec((None, d, tn), lambda i, j: (layer, 0, j))
    o_spec = pl.BlockSpec((tm, tn), lambda i, j: (i, j))
    scratch = [pltpu.VMEM((tm, d), BF16)]
    if gh is not None:
        assert n_norm % tn == 0 and tn % hd == 0 and gh.shape[1] == n
        gh_spec = pl.BlockSpec((1, tn), lambda i, j: (0, j))
        return pl.pallas_call(
            functools.partial(_norm_matmul_headnorm_kernel, norm_blocks=n_norm // tn, hd=hd),
            out_shape=jax.ShapeDtypeStruct((t, n), BF16),
            grid=grid, in_specs=[x_spec, g_spec, w_spec, gh_spec], out_specs=o_spec,
            scratch_shapes=scratch, compiler_params=_params("parallel", "arbitrary"),
            name="norm_matmul_headnorm",
        )(x, g, w, gh)
    nz = wz.shape[1]
    wz_spec = pl.BlockSpec((d, nz), lambda i, j: (0, 0))
    z_spec = pl.BlockSpec((tm, nz), lambda i, j: (i, 0))
    return pl.pallas_call(
        _norm_matmul_z_kernel,
        out_shape=(jax.ShapeDtypeStruct((t, n), BF16), jax.ShapeDtypeStruct((t, nz), F32)),
        grid=grid, in_specs=[x_spec, g_spec, w_spec, wz_spec], out_specs=(o_spec, z_spec),
        scratch_shapes=scratch, compiler_params=_params("parallel", "arbitrary"),
        name="norm_matmul_z",
    )(x, g, w, wz)


def _matmul_res_kernel(a_ref, w_ref, r_ref, o_ref):
    o_ref[...] = r_ref[...] + jnp.dot(a_ref[...], w_ref[...], preferred_element_type=F32)


def _matmul_res(a, w, layer, res, *, tm=512, tn=2048):
    t, k = a.shape
    n = w.shape[2]
    tm, tn = min(tm, t), min(tn, n)
    return pl.pallas_call(
        _matmul_res_kernel,
        out_shape=jax.ShapeDtypeStruct((t, n), F32),
        grid=(t // tm, n // tn),
        in_specs=[pl.BlockSpec((tm, k), lambda i, j: (i, 0)),
                  pl.BlockSpec((None, k, tn), lambda i, j: (layer, 0, j)),
                  pl.BlockSpec((tm, tn), lambda i, j: (i, j))],
        out_specs=pl.BlockSpec((tm, tn), lambda i, j: (i, j)),
        compiler_params=_params("parallel", "parallel"),
        name="matmul_res",
    )(a, w, res)


def _mlp_kernel(x_ref, g_ref, wu_ref, wd_ref, o_ref, xn_ref):
    @pl.when(pl.program_id(1) == 0)
    def _():
        _rmsnorm_rows_to(x_ref, g_ref, xn_ref, copy_ref=o_ref)

    h = jnp.dot(xn_ref[...], wu_ref[...], preferred_element_type=F32)
    h = jnp.square(jnp.maximum(h, 0.0)).astype(BF16)
    o_ref[...] += jnp.dot(h, wd_ref[...], preferred_element_type=F32)


def _mlp(x, g, w_up, w_down, layer, *, tm=512, tf=1024):
    t, d = x.shape
    f = w_up.shape[2]
    tm, tf = min(tm, t), min(tf, f)
    return pl.pallas_call(
        _mlp_kernel,
        out_shape=jax.ShapeDtypeStruct((t, d), F32),
        grid=(t // tm, f // tf),
        in_specs=[pl.BlockSpec((tm, d), lambda i, j: (i, 0)),
                  pl.BlockSpec((1, d), lambda i, j: (0, 0)),
                  pl.BlockSpec((None, d, tf), lambda i, j: (layer, 0, j)),
                  pl.BlockSpec((None, tf, d), lambda i, j: (layer, j, 0))],
        out_specs=pl.BlockSpec((tm, d), lambda i, j: (i, 0)),
        scratch_shapes=[pltpu.VMEM((tm, d), BF16)],
        compiler_params=_params("parallel", "arbitrary"),
        name="mlp",
    )(x, g, w_up, w_down)


def _split_bf16(a):
    hi = a.astype(BF16)
    lo = (a - hi.astype(F32)).astype(BF16)
    return hi, lo


def _dot(a, b):
    return jnp.dot(a, b, preferred_element_type=F32)


def _dot_tn(a, b):
    return lax.dot_general(a, b, (((0,), (0,)), ((), ())), preferred_element_type=F32)


def _dot_nt(a, b):
    return lax.dot_general(a, b, (((1,), (1,)), ((), ())), preferred_element_type=F32)


def _gla_kernel(q_ref, k_ref, v_ref, r_ref, z_ref, wgh_ref, wgl_ref, bg_ref, go_ref, tri_ref,
                o_ref, state_ref):
    hb = state_ref.shape[0]

    @pl.when(pl.program_id(2) == 0)
    def _():
        state_ref[...] = jnp.zeros_like(state_ref)

    z_parts = _split_bf16(z_ref[...])
    tri = tri_ref[...]
    _interleave([_gla_head(h, q_ref, k_ref, v_ref, r_ref, z_parts, wgh_ref, wgl_ref, bg_ref, go_ref,
                           tri, o_ref, state_ref) for h in range(hb)], hb)


def _gla_head(h, q_ref, k_ref, v_ref, r_ref, z_parts, wgh_ref, wgl_ref, bg_ref, go_ref, tri,
              o_ref, state_ref):
    rows = q_ref.shape[0]
    hb = state_ref.shape[0]
    dk = q_ref.shape[1] // hb
    dv = v_ref.shape[1] // hb
    ks = slice(h * dk, (h + 1) * dk)
    vs = slice(h * dv, (h + 1) * dv)
    nc = rows // CHUNK
    chunk = lambda c: slice(c * CHUNK, (c + 1) * CHUNK)

    z_hi, z_lo = z_parts
    wg_hi = wgh_ref[:, ks]
    logit = _dot(z_hi, wg_hi) + _dot(z_lo, wg_hi) + _dot(z_hi, wgl_ref[:, ks]) + bg_ref[:, ks]
    t = logit * LOG2E
    yield
    log_a = (jnp.minimum(t, 0.0) - jnp.log2(1.0 + jnp.exp2(-jnp.abs(t)))) * (1.0 / GLA_TAU)

    la_hi, la_lo = _split_bf16(log_a)
    b = _dot(tri, la_hi) + _dot(tri, la_lo)
    yield
    bl = [b[(c + 1) * CHUNK - 1:(c + 1) * CHUNK, :] for c in range(nc)]

    def dec(c1, c2):
        return jnp.exp2(functools.reduce(jnp.add, bl[c1:c2]))

    q_dec = q_ref[:, ks].astype(F32) * (dk ** -0.5) * jnp.exp2(b)
    k = k_ref[:, ks].astype(F32)
    k_inv = (k * jnp.exp2(-b)).astype(BF16)
    k_end = [k[chunk(c)] * jnp.exp2(bl[c] - b[chunk(c)]) for c in range(nc)]
    q_dec_bf = q_dec.astype(BF16)
    yield

    trow = lax.broadcasted_iota(jnp.int32, (CHUNK, rows), 0)
    scol = lax.broadcasted_iota(jnp.int32, (CHUNK, rows), 1)
    a_rows = []
    for c in range(nc):
        keys = [(k_end[cp] * dec(cp + 1, c) if cp + 1 < c else k_end[cp]).astype(BF16)
                for cp in range(c)]
        keys.append(k_inv[chunk(c)])
        if c + 1 < nc:
            keys.append(jnp.zeros(((nc - 1 - c) * CHUNK, dk), BF16))
        s = _dot_nt(q_dec_bf[chunk(c)], jnp.concatenate(keys, axis=0))
        a_rows.append(jnp.where(scol <= trow + c * CHUNK, s, 0.0).astype(BF16))
    a = jnp.concatenate(a_rows, axis=0)
    yield

    q_in = jnp.concatenate(
        [(q_dec[chunk(c)] * dec(0, c)).astype(BF16) if c else q_dec_bf[chunk(0)] for c in range(nc)],
        axis=0)
    k_out = jnp.concatenate(
        [(k_end[c] * dec(c + 1, nc) if c + 1 < nc else k_end[c]).astype(BF16) for c in range(nc)],
        axis=0)

    v = v_ref[:, vs]
    state = state_ref[h]
    o = _dot(a, v) + _dot_nt(q_in, state.astype(BF16))
    state_ref[h] = state * dec(0, nc) + _dot_tn(v, k_out)
    yield

    for c in range(nc):
        oc = o[chunk(c)]
        ms = jnp.mean(oc * oc, axis=-1, keepdims=True)
        r = r_ref[chunk(c), vs].astype(F32)
        y = (oc * lax.rsqrt(ms + EPS)) * go_ref[:, vs]
        o_ref[chunk(c), vs] = (y * (r * (jnp.tanh(0.5 * r) + 1.0))).astype(o_ref.dtype)


def _gla_core(proj, z, wg, bg, g_half, batch, seq):
    t = proj.shape[0]
    d = g_half.shape[1]
    dv = d // GLA_HEADS
    dk = dv // 2
    rows = min(GLA_ROWS, seq)
    nblk = seq // rows
    hb = GLA_HB
    ng = GLA_HEADS // hb
    wk, wv = hb * dk, hb * dv
    k_off = ng
    v_off = (2 * GLA_HEADS * dk) // wv
    r_off = v_off + ng
    assert (2 * GLA_HEADS * dk) % wv == 0
    wg_hi, wg_lo = _split_bf16(wg)
    idx = jnp.arange(rows)
    tri = ((idx[:, None] // CHUNK == idx[None, :] // CHUNK) & (idx[None, :] <= idx[:, None])).astype(BF16)
    rowmap = lambda b, h, n: b * nblk + n
    wg_spec = pl.BlockSpec((wg.shape[0], wk), lambda b, h, n: (0, h))
    return pl.pallas_call(
        _gla_kernel,
        out_shape=jax.ShapeDtypeStruct((t, d), BF16),
        grid=(batch, ng, nblk),
        in_specs=[pl.BlockSpec((rows, wk), lambda b, h, n: (rowmap(b, h, n), h)),
                  pl.BlockSpec((rows, wk), lambda b, h, n: (rowmap(b, h, n), k_off + h)),
                  pl.BlockSpec((rows, wv), lambda b, h, n: (rowmap(b, h, n), v_off + h)),
                  pl.BlockSpec((rows, wv), lambda b, h, n: (rowmap(b, h, n), r_off + h)),
                  pl.BlockSpec((rows, z.shape[1]), lambda b, h, n: (rowmap(b, h, n), 0)),
                  wg_spec, wg_spec,
                  pl.BlockSpec((1, wk), lambda b, h, n: (0, h)),
                  pl.BlockSpec((1, wv), lambda b, h, n: (0, h)),
                  pl.BlockSpec((rows, rows), lambda b, h, n: (0, 0))],
        out_specs=pl.BlockSpec((rows, wv), lambda b, h, n: (rowmap(b, h, n), h)),
        scratch_shapes=[pltpu.VMEM((hb, dv, dk), F32)],
        compiler_params=_params("parallel", "parallel", "arbitrary"),
        name="gla_core",
    )(proj, proj, proj, proj, z, wg_hi, wg_lo, bg, g_half, tri)


def _attn_kernel(q_ref, k0_ref, k1_ref, k2_ref, v0_ref, v1_ref, v2_ref, gen_ref, o_ref, bias_ref):
    qb = q_ref.shape[0]
    hb = bias_ref.shape[0]
    hd = q_ref.shape[1] // hb
    nkb = ATT_KB // qb
    cpb = qb // CHUNK
    i = pl.program_id(2)
    k_refs = (k0_ref, k1_ref, k2_ref)
    v_refs = (v0_ref, v1_ref, v2_ref)

    @pl.when((pl.program_id(1) == 0) & (i == 0))
    def _():
        cq = lax.broadcasted_iota(jnp.int32, (qb, qb), 0) // CHUNK
        ck = lax.broadcasted_iota(jnp.int32, (qb, qb), 1) // CHUNK
        for j in range(nkb):
            band = (ck + cpb * j >= cq) & (ck + cpb * j <= cq + LEFT_CHUNKS)
            for h in range(hb):
                gen = jnp.broadcast_to(gen_ref[h, j:j + 1, :], (qb, 2 * qb))
                toe = pltpu.roll(gen, qb + 1, 1, stride=1, stride_axis=0)[:, :qb]
                bias_ref[h, :, j * qb:(j + 1) * qb] = jnp.where(band, toe, NEG)

    ones = jnp.ones((qb, hd), BF16)
    q_ext = jnp.full((qb, hd), 1.0 / hd, BF16)
    k_ext = jnp.concatenate(
        [jnp.full((qb, hd), jnp.where(i >= nkb - 1 - j, 0.0, NEG), F32).astype(BF16)
         for j in range(nkb)], axis=0)
    def head(h):
        cs = slice(h * hd, (h + 1) * hd)
        q = jnp.concatenate([q_ref[:, cs], q_ext], axis=1)
        k = jnp.concatenate([jnp.concatenate([r[:, cs] for r in k_refs], axis=0), k_ext], axis=1)
        s = _dot_nt(q, k) + bias_ref[h]
        yield
        m = jnp.max(s, axis=-1, keepdims=True)
        yield
        p = jnp.exp2(s - m).astype(BF16)
        yield
        v1 = jnp.concatenate([jnp.concatenate([r[:, cs] for r in v_refs], axis=0),
                              jnp.concatenate([ones] * nkb, axis=0)], axis=1)
        acc = _dot(p, v1)
        yield
        o_ref[:, cs] = (acc[:, :hd] / acc[:, hd:]).astype(o_ref.dtype)

    _interleave([head(h) for h in range(hb)], ATT_INTERLEAVE)


def _attn_bias_generators(rel_bias, qb):
    left = LEFT_CHUNKS * CHUNK
    x = jnp.arange(2 * qb)[None, :]
    j = jnp.arange((qb + left) // qb)[:, None]
    dist = left - qb * j + (qb - 1) - x
    idx = jnp.clip(dist, -(CHUNK - 1), REL_CLIP) + (CHUNK - 1)
    return rel_bias[:, idx].astype(F32) * LOG2E


def _attn_core(qkv, rel_bias, batch, seq):
    t = qkv.shape[0]
    d = qkv.shape[1] // 3
    hd = d // ATT_HEADS
    qb = ATT_QB
    assert seq % qb == 0 and ATT_KB % qb == 0 and ATT_KB // qb == 3
    nq = seq // qb
    hb = ATT_HB
    ng = ATT_HEADS // hb
    gen = _attn_bias_generators(rel_bias, qb)
    w = hb * hd

    def kv_spec(off, back):
        return pl.BlockSpec((qb, w), lambda g, b, i: (b * nq + jnp.maximum(i - back, 0), off * ng + g))

    return pl.pallas_call(
        _attn_kernel,
        out_shape=jax.ShapeDtypeStruct((t, d), BF16),
        grid=(ng, batch, nq),
        in_specs=[pl.BlockSpec((qb, w), lambda g, b, i: (b * nq + i, g)),
                  kv_spec(1, 2), kv_spec(1, 1), kv_spec(1, 0),
                  kv_spec(2, 2), kv_spec(2, 1), kv_spec(2, 0),
                  pl.BlockSpec((hb,) + gen.shape[1:], lambda g, b, i: (g, 0, 0))],
        out_specs=pl.BlockSpec((qb, w), lambda g, b, i: (b * nq + i, g)),
        scratch_shapes=[pltpu.VMEM((hb, qb, ATT_KB), F32)],
        compiler_params=_params("arbitrary", "arbitrary", "arbitrary"),
        name="attn_core",
    )(qkv, qkv, qkv, qkv, qkv, qkv, qkv, gen)


def kernel(x, norm_mix_g, norm_mlp_g, gla_w_in, gla_w_gate_up, gla_b_gate, gla_g_out, gla_w_out,
           att_w_in, att_g_q, att_g_k, att_rel_bias, att_w_out, mlp_w_up, mlp_w_down):
    batch, seq, d = x.shape
    depth = norm_mix_g.shape[0]
    xf = x.reshape(batch * seq, d)
    gla_w_in_bf, gla_w_out_bf = gla_w_in.astype(BF16), gla_w_out.astype(BF16)
    att_w_in_bf, att_w_out_bf = att_w_in.astype(BF16), att_w_out.astype(BF16)
    mlp_w_up_bf, mlp_w_down_bf = mlp_w_up.astype(BF16), mlp_w_down.astype(BF16)
    for i in range(depth):
        j = i // N_MIXERS
        g_mix = norm_mix_g[i].reshape(1, d)
        if i % N_MIXERS == 0:
            n_main = gla_w_in.shape[2] - GLA_GATE_RANK
            w_z = jnp.pad(gla_w_in[j, :, n_main:], ((0, 0), (0, LANES - GLA_GATE_RANK))).astype(BF16)
            wg = jnp.pad(gla_w_gate_up[j], ((0, LANES - GLA_GATE_RANK), (0, 0)))
            proj, z = _norm_matmul(xf, g_mix, gla_w_in_bf, j, n_main, wz=w_z)
            mix = _gla_core(proj, z, wg, gla_b_gate[j].reshape(1, -1),
                            0.5 * gla_g_out[j].reshape(1, d), batch, seq)
            w_out = gla_w_out_bf
        else:
            hd = att_g_q.shape[1]
            gh = jnp.concatenate([jnp.tile(att_g_q[j] * (hd ** -0.5 * LOG2E), ATT_HEADS),
                                  jnp.tile(att_g_k[j], ATT_HEADS),
                                  jnp.ones((d,), F32)]).reshape(1, -1)
            qkv = _norm_matmul(xf, g_mix, att_w_in_bf, j, 3 * d, gh=gh, hd=hd, n_norm=2 * d)
            mix = _attn_core(qkv, att_rel_bias[j], batch, seq)
            w_out = att_w_out_bf
        xf = _matmul_res(mix, w_out, j, xf)
        xf = _mlp(xf, norm_mlp_g[i].reshape(1, d), mlp_w_up_bf, mlp_w_down_bf, i)
    return xf.reshape(batch, seq, d)
```

```python
import functools

import jax
import jax.numpy as jnp
from jax import lax
from jax.experimental import pallas as pl
from jax.experimental.pallas import tpu as pltpu

F32 = jnp.float32
BF16 = jnp.bfloat16

EPS = 1e-6
CHUNK = 64
N_MIXERS = 2
GLA_HEADS = 4
GLA_GATE_RANK = 16
GLA_TAU = 16.0
ATT_HEADS = 16
LEFT_CHUNKS = 8
REL_CLIP = 256

LANES = 128
NORM_ROWS = 16
NORM_UNROLL = 8
HEADNORM_SPLIT = 4
VMEM_LIMIT = 56 * 1024 * 1024
NEG = -1e30
LOG2E = 1.4426950408889634

GLA_ROWS = 256
GLA_HB = 4
ATT_QB = 256
ATT_KB = ATT_QB + LEFT_CHUNKS * CHUNK
ATT_HB = 16
ATT_INTERLEAVE = 1


def _params(*sem):
    return pltpu.CompilerParams(dimension_semantics=sem, vmem_limit_bytes=VMEM_LIMIT)


def _interleave(chains, width):
    for start in range(0, len(chains), width):
        group = chains[start:start + width]
        while group:
            alive = []
            for chain in group:
                try:
                    next(chain)
                    alive.append(chain)
                except StopIteration:
                    pass
            group = alive


def _rmsnorm_rows_to(x_ref, g_ref, dst_ref, copy_ref=None):
    rows = x_ref.shape[0]

    def body(r, carry):
        sl = pl.ds(pl.multiple_of(r * NORM_ROWS, NORM_ROWS), NORM_ROWS)
        x = x_ref[sl, :]
        ms = jnp.mean(x * x, axis=-1, keepdims=True)
        dst_ref[sl, :] = ((x * lax.rsqrt(ms + EPS)) * g_ref[...]).astype(dst_ref.dtype)
        if copy_ref is not None:
            copy_ref[sl, :] = x
        return carry

    lax.fori_loop(0, rows // NORM_ROWS, body, 0, unroll=NORM_UNROLL)


def _norm_matmul_headnorm_kernel(x_ref, g_ref, w_ref, gh_ref, o_ref, xn_ref, *, norm_blocks, hd):
    j = pl.program_id(1)

    @pl.when(j == 0)
    def _():
        _rmsnorm_rows_to(x_ref, g_ref, xn_ref)

    normed = j < norm_blocks
    rows = xn_ref.shape[0] // HEADNORM_SPLIT
    for part in range(HEADNORM_SPLIT):
        rs = slice(part * rows, (part + 1) * rows)
        acc = jnp.dot(xn_ref[rs, :], w_ref[...], preferred_element_type=F32)
        for h in range(acc.shape[1] // hd):
            cs = slice(h * hd, (h + 1) * hd)
            a = acc[:, cs]
            ms = jnp.mean(a * a, axis=-1, keepdims=True)
            scale = jnp.where(normed, lax.rsqrt(ms + EPS), 1.0)
            o_ref[rs, cs] = ((a * scale) * gh_ref[:, cs]).astype(o_ref.dtype)


def _norm_matmul_z_kernel(x_ref, g_ref, w_ref, wz_ref, o_ref, z_ref, xn_ref):
    @pl.when(pl.program_id(1) == 0)
    def _():
        _rmsnorm_rows_to(x_ref, g_ref, xn_ref)
        z_ref[...] = jnp.dot(xn_ref[...], wz_ref[...], preferred_element_type=F32)

    rows = xn_ref.shape[0] // HEADNORM_SPLIT
    for part in range(HEADNORM_SPLIT):
        rs = slice(part * rows, (part + 1) * rows)
        o_ref[rs, :] = jnp.dot(xn_ref[rs, :], w_ref[...], preferred_element_type=F32).astype(o_ref.dtype)


def _norm_matmul(x, g, w, layer, n, wz=None, gh=None, hd=None, n_norm=0, *, tm=1024, tn=2048):
    t, d = x.shape
    tm, tn = min(tm, t), min(tn, n)
    grid = (t // tm, n // tn)
    x_spec = pl.BlockSpec((tm, d), lambda i, j: (i, 0))
    g_spec = pl.BlockSpec((1, d), lambda i, j: (0, 0))
    w_spec = pl.BlockSpec((None, d, tn), lambda i, j: (layer, 0, j))
    o_spec = pl.BlockSpec((tm, tn), lambda i, j: (i, j))
    scratch = [pltpu.VMEM((tm, d), BF16)]
    if gh is not None:
        assert n_norm % tn == 0 and tn % hd == 0 and gh.shape[1] == n
        gh_spec = pl.BlockSpec((1, tn), lambda i, j: (0, j))
        return pl.pallas_call(
            functools.partial(_norm_matmul_headnorm_kernel, norm_blocks=n_norm // tn, hd=hd),
            out_shape=jax.ShapeDtypeStruct((t, n), BF16),
            grid=grid, in_specs=[x_spec, g_spec, w_spec, gh_spec], out_specs=o_spec,
            scratch_shapes=scratch, compiler_params=_params("parallel", "arbitrary"),
            name="norm_matmul_headnorm",
        )(x, g, w, gh)
    nz = wz.shape[1]
    wz_spec = pl.BlockSpec((d, nz), lambda i, j: (0, 0))
    z_spec = pl.BlockSpec((tm, nz), lambda i, j: (i, 0))
    return pl.pallas_call(
        _norm_matmul_z_kernel,
        out_shape=(jax.ShapeDtypeStruct((t, n), BF16), jax.ShapeDtypeStruct((t, nz), F32)),
        grid=grid, in_specs=[x_spec, g_spec, w_spec, wz_spec], out_specs=(o_spec, z_spec),
        scratch_shapes=scratch, compiler_params=_params("parallel", "arbitrary"),
        name="norm_matmul_z",
    )(x, g, w, wz)


def _matmul_res_kernel(a_ref, w_ref, r_ref, o_ref):
    o_ref[...] = r_ref[...] + jnp.dot(a_ref[...], w_ref[...], preferred_element_type=F32)


def _matmul_res(a, w, layer, res, *, tm=512, tn=2048):
    t, k = a.shape
    n = w.shape[2]
    tm, tn = min(tm, t), min(tn, n)
    return pl.pallas_call(
        _matmul_res_kernel,
        out_shape=jax.ShapeDtypeStruct((t, n), F32),
        grid=(t // tm, n // tn),
        in_specs=[pl.BlockSpec((tm, k), lambda i, j: (i, 0)),
                  pl.BlockSpec((None, k, tn), lambda i, j: (layer, 0, j)),
                  pl.BlockSpec((tm, tn), lambda i, j: (i, j))],
        out_specs=pl.BlockSpec((tm, tn), lambda i, j: (i, j)),
        compiler_params=_params("parallel", "parallel"),
        name="matmul_res",
    )(a, w, res)


def _mlp_kernel(x_ref, g_ref, wu_ref, wd_ref, o_ref, xn_ref):
    @pl.when(pl.program_id(1) == 0)
    def _():
        _rmsnorm_rows_to(x_ref, g_ref, xn_ref, copy_ref=o_ref)

    h = jnp.dot(xn_ref[...], wu_ref[...], preferred_element_type=F32)
    h = jnp.square(jnp.maximum(h, 0.0)).astype(BF16)
    o_ref[...] += jnp.dot(h, wd_ref[...], preferred_element_type=F32)


def _mlp(x, g, w_up, w_down, layer, *, tm=512, tf=1024):
    t, d = x.shape
    f = w_up.shape[2]
    tm, tf = min(tm, t), min(tf, f)
    return pl.pallas_call(
        _mlp_kernel,
        out_shape=jax.ShapeDtypeStruct((t, d), F32),
        grid=(t // tm, f // tf),
        in_specs=[pl.BlockSpec((tm, d), lambda i, j: (i, 0)),
                  pl.BlockSpec((1, d), lambda i, j: (0, 0)),
                  pl.BlockSpec((None, d, tf), lambda i, j: (layer, 0, j)),
                  pl.BlockSpec((None, tf, d), lambda i, j: (layer, j, 0))],
        out_specs=pl.BlockSpec((tm, d), lambda i, j: (i, 0)),
        scratch_shapes=[pltpu.VMEM((tm, d), BF16)],
        compiler_params=_params("parallel", "arbitrary"),
        name="mlp",
    )(x, g, w_up, w_down)


def _split_bf16(a):
    hi = a.astype(BF16)
    lo = (a - hi.astype(F32)).astype(BF16)
    return hi, lo


def _dot(a, b):
    return jnp.dot(a, b, preferred_element_type=F32)


def _dot_tn(a, b):
    return lax.dot_general(a, b, (((0,), (0,)), ((), ())), preferred_element_type=F32)


def _dot_nt(a, b):
    return lax.dot_general(a, b, (((1,), (1,)), ((), ())), preferred_element_type=F32)


def _gla_kernel(q_ref, k_ref, v_ref, r_ref, z_ref, wgh_ref, wgl_ref, bg_ref, go_ref, tri_ref,
                o_ref, state_ref):
    hb = state_ref.shape[0]

    @pl.when(pl.program_id(2) == 0)
    def _():
        state_ref[...] = jnp.zeros_like(state_ref)

    z_parts = _split_bf16(z_ref[...])
    tri = tri_ref[...]
    _interleave([_gla_head(h, q_ref, k_ref, v_ref, r_ref, z_parts, wgh_ref, wgl_ref, bg_ref, go_ref,
                           tri, o_ref, state_ref) for h in range(hb)], hb)


def _gla_head(h, q_ref, k_ref, v_ref, r_ref, z_parts, wgh_ref, wgl_ref, bg_ref, go_ref, tri,
              o_ref, state_ref):
    rows = q_ref.shape[0]
    hb = state_ref.shape[0]
    dk = q_ref.shape[1] // hb
    dv = v_ref.shape[1] // hb
    ks = slice(h * dk, (h + 1) * dk)
    vs = slice(h * dv, (h + 1) * dv)
    nc = rows // CHUNK
    chunk = lambda c: slice(c * CHUNK, (c + 1) * CHUNK)

    z_hi, z_lo = z_parts
    wg_hi = wgh_ref[:, ks]
    logit = _dot(z_hi, wg_hi) + _dot(z_lo, wg_hi) + _dot(z_hi, wgl_ref[:, ks]) + bg_ref[:, ks]
    t = logit * LOG2E
    yield
    log_a = (jnp.minimum(t, 0.0) - jnp.log2(1.0 + jnp.exp2(-jnp.abs(t)))) * (1.0 / GLA_TAU)

    la_hi, la_lo = _split_bf16(log_a)
    b = _dot(tri, la_hi) + _dot(tri, la_lo)
    yield
    bl = [b[(c + 1) * CHUNK - 1:(c + 1) * CHUNK, :] for c in range(nc)]

    def dec(c1, c2):
        return jnp.exp2(functools.reduce(jnp.add, bl[c1:c2]))

    q_dec = q_ref[:, ks].astype(F32) * (dk ** -0.5) * jnp.exp2(b)
    k = k_ref[:, ks].astype(F32)
    k_inv = (k * jnp.exp2(-b)).astype(BF16)
    k_end = [k[chunk(c)] * jnp.exp2(bl[c] - b[chunk(c)]) for c in range(nc)]
    q_dec_bf = q_dec.astype(BF16)
    yield

    trow = lax.broadcasted_iota(jnp.int32, (CHUNK, rows), 0)
    scol = lax.broadcasted_iota(jnp.int32, (CHUNK, rows), 1)
    a_rows = []
    for c in range(nc):
        keys = [(k_end[cp] * dec(cp + 1, c) if cp + 1 < c else k_end[cp]).astype(BF16)
                for cp in range(c)]
        keys.append(k_inv[chunk(c)])
        if c + 1 < nc:
            keys.append(jnp.zeros(((nc - 1 - c) * CHUNK, dk), BF16))
        s = _dot_nt(q_dec_bf[chunk(c)], jnp.concatenate(keys, axis=0))
        a_rows.append(jnp.where(scol <= trow + c * CHUNK, s, 0.0).astype(BF16))
    a = jnp.concatenate(a_rows, axis=0)
    yield

    q_in = jnp.concatenate(
        [(q_dec[chunk(c)] * dec(0, c)).astype(BF16) if c else q_dec_bf[chunk(0)] for c in range(nc)],
        axis=0)
    k_out = jnp.concatenate(
        [(k_end[c] * dec(c + 1, nc) if c + 1 < nc else k_end[c]).astype(BF16) for c in range(nc)],
        axis=0)

    v = v_ref[:, vs]
    state = state_ref[h]
    o = _dot(a, v) + _dot_nt(q_in, state.astype(BF16))
    state_ref[h] = state * dec(0, nc) + _dot_tn(v, k_out)
    yield

    for c in range(nc):
        oc = o[chunk(c)]
        ms = jnp.mean(oc * oc, axis=-1, keepdims=True)
        r = r_ref[chunk(c), vs].astype(F32)
        y = (oc * lax.rsqrt(ms + EPS)) * go_ref[:, vs]
        o_ref[chunk(c), vs] = (y * (r * (jnp.tanh(0.5 * r) + 1.0))).astype(o_ref.dtype)


def _gla_core(proj, z, wg, bg, g_half, batch, seq):
    t = proj.shape[0]
    d = g_half.shape[1]
    dv = d // GLA_HEADS
    dk = dv // 2
    rows = min(GLA_ROWS, seq)
    nblk = seq // rows
    hb = GLA_HB
    ng = GLA_HEADS // hb
    wk, wv = hb * dk, hb * dv
    k_off = ng
    v_off = (2 * GLA_HEADS * dk) // wv
    r_off = v_off + ng
    assert (2 * GLA_HEADS * dk) % wv == 0
    wg_hi, wg_lo = _split_bf16(wg)
    idx = jnp.arange(rows)
    tri = ((idx[:, None] // CHUNK == idx[None, :] // CHUNK) & (idx[None, :] <= idx[:, None])).astype(BF16)
    rowmap = lambda b, h, n: b * nblk + n
    wg_spec = pl.BlockSpec((wg.shape[0], wk), lambda b, h, n: (0, h))
    return pl.pallas_call(
        _gla_kernel,
        out_shape=jax.ShapeDtypeStruct((t, d), BF16),
        grid=(batch, ng, nblk),
        in_specs=[pl.BlockSpec((rows, wk), lambda b, h, n: (rowmap(b, h, n), h)),
                  pl.BlockSpec((rows, wk), lambda b, h, n: (rowmap(b, h, n), k_off + h)),
                  pl.BlockSpec((rows, wv), lambda b, h, n: (rowmap(b, h, n), v_off + h)),
                  pl.BlockSpec((rows, wv), lambda b, h, n: (rowmap(b, h, n), r_off + h)),
                  pl.BlockSpec((rows, z.shape[1]), lambda b, h, n: (rowmap(b, h, n), 0)),
                  wg_spec, wg_spec,
                  pl.BlockSpec((1, wk), lambda b, h, n: (0, h)),
                  pl.BlockSpec((1, wv), lambda b, h, n: (0, h)),
                  pl.BlockSpec((rows, rows), lambda b, h, n: (0, 0))],
        out_specs=pl.BlockSpec((rows, wv), lambda b, h, n: (rowmap(b, h, n), h)),
        scratch_shapes=[pltpu.VMEM((hb, dv, dk), F32)],
        compiler_params=_params("parallel", "parallel", "arbitrary"),
        name="gla_core",
    )(proj, proj, proj, proj, z, wg_hi, wg_lo, bg, g_half, tri)


def _attn_kernel(q_ref, k0_ref, k1_ref, k2_ref, v0_ref, v1_ref, v2_ref, gen_ref, o_ref, bias_ref):
    qb = q_ref.shape[0]
    hb = bias_ref.shape[0]
    hd = q_ref.shape[1] // hb
    nkb = ATT_KB // qb
    cpb = qb // CHUNK
    i = pl.program_id(2)
    k_refs = (k0_ref, k1_ref, k2_ref)
    v_refs = (v0_ref, v1_ref, v2_ref)

    @pl.when((pl.program_id(1) == 0) & (i == 0))
    def _():
        cq = lax.broadcasted_iota(jnp.int32, (qb, qb), 0) // CHUNK
        ck = lax.broadcasted_iota(jnp.int32, (qb, qb), 1) // CHUNK
        for j in range(nkb):
            band = (ck + cpb * j >= cq) & (ck + cpb * j <= cq + LEFT_CHUNKS)
            for h in range(hb):
                gen = jnp.broadcast_to(gen_ref[h, j:j + 1, :], (qb, 2 * qb))
                toe = pltpu.roll(gen, qb + 1, 1, stride=1, stride_axis=0)[:, :qb]
                bias_ref[h, :, j * qb:(j + 1) * qb] = jnp.where(band, toe, NEG)

    ones = jnp.ones((qb, hd), BF16)
    q_ext = jnp.full((qb, hd), 1.0 / hd, BF16)
    k_ext = jnp.concatenate(
        [jnp.full((qb, hd), jnp.where(i >= nkb - 1 - j, 0.0, NEG), F32).astype(BF16)
         for j in range(nkb)], axis=0)
    def head(h):
        cs = slice(h * hd, (h + 1) * hd)
        q = jnp.concatenate([q_ref[:, cs], q_ext], axis=1)
        k = jnp.concatenate([jnp.concatenate([r[:, cs] for r in k_refs], axis=0), k_ext], axis=1)
        s = _dot_nt(q, k) + bias_ref[h]
        yield
        m = jnp.max(s, axis=-1, keepdims=True)
        yield
        p = jnp.exp2(s - m).astype(BF16)
        yield
        v1 = jnp.concatenate([jnp.concatenate([r[:, cs] for r in v_refs], axis=0),
                              jnp.concatenate([ones] * nkb, axis=0)], axis=1)
        acc = _dot(p, v1)
        yield
        o_ref[:, cs] = (acc[:, :hd] / acc[:, hd:]).astype(o_ref.dtype)

    _interleave([head(h) for h in range(hb)], ATT_INTERLEAVE)


def _attn_bias_generators(rel_bias, qb):
    left = LEFT_CHUNKS * CHUNK
    x = jnp.arange(2 * qb)[None, :]
    j = jnp.arange((qb + left) // qb)[:, None]
    dist = left - qb * j + (qb - 1) - x
    idx = jnp.clip(dist, -(CHUNK - 1), REL_CLIP) + (CHUNK - 1)
    return rel_bias[:, idx].astype(F32) * LOG2E


def _attn_core(qkv, rel_bias, batch, seq):
    t = qkv.shape[0]
    d = qkv.shape[1] // 3
    hd = d // ATT_HEADS
    qb = ATT_QB
    assert seq % qb == 0 and ATT_KB % qb == 0 and ATT_KB // qb == 3
    nq = seq // qb
    hb = ATT_HB
    ng = ATT_HEADS // hb
    gen = _attn_bias_generators(rel_bias, qb)
    w = hb * hd

    def kv_spec(off, back):
        return pl.BlockSpec((qb, w), lambda g, b, i: (b * nq + jnp.maximum(i - back, 0), off * ng + g))

    return pl.pallas_call(
        _attn_kernel,
        out_shape=jax.ShapeDtypeStruct((t, d), BF16),
        grid=(ng, batch, nq),
        in_specs=[pl.BlockSpec((qb, w), lambda g, b, i: (b * nq + i, g)),
                  kv_spec(1, 2), kv_spec(1, 1), kv_spec(1, 0),
                  kv_spec(2, 2), kv_spec(2, 1), kv_spec(2, 0),
                  pl.BlockSpec((hb,) + gen.shape[1:], lambda g, b, i: (g, 0, 0))],
        out_specs=pl.BlockSpec((qb, w), lambda g, b, i: (b * nq + i, g)),
        scratch_shapes=[pltpu.VMEM((hb, qb, ATT_KB), F32)],
        compiler_params=_params("arbitrary", "arbitrary", "arbitrary"),
        name="attn_core",
    )(qkv, qkv, qkv, qkv, qkv, qkv, qkv, gen)


def kernel(x, norm_mix_g, norm_mlp_g, gla_w_in, gla_w_gate_up, gla_b_gate, gla_g_out, gla_w_out,
           att_w_in, att_g_q, att_g_k, att_rel_bias, att_w_out, mlp_w_up, mlp_w_down):
    batch, seq, d = x.shape
    depth = norm_mix_g.shape[0]
    xf = x.reshape(batch * seq, d)
    gla_w_in_bf, gla_w_out_bf = gla_w_in.astype(BF16), gla_w_out.astype(BF16)
    att_w_in_bf, att_w_out_bf = att_w_in.astype(BF16), att_w_out.astype(BF16)
    mlp_w_up_bf, mlp_w_down_bf = mlp_w_up.astype(BF16), mlp_w_down.astype(BF16)
    for i in range(depth):
        j = i // N_MIXERS
        g_mix = norm_mix_g[i].reshape(1, d)
        if i % N_MIXERS == 0:
            n_main = gla_w_in.shape[2] - GLA_GATE_RANK
            w_z = jnp.pad(gla_w_in[j, :, n_main:], ((0, 0), (0, LANES - GLA_GATE_RANK))).astype(BF16)
            wg = jnp.pad(gla_w_gate_up[j], ((0, LANES - GLA_GATE_RANK), (0, 0)))
            proj, z = _norm_matmul(xf, g_mix, gla_w_in_bf, j, n_main, wz=w_z)
            mix = _gla_core(proj, z, wg, gla_b_gate[j].reshape(1, -1),
                            0.5 * gla_g_out[j].reshape(1, d), batch, seq)
            w_out = gla_w_out_bf
        else:
            hd = att_g_q.shape[1]
            gh = jnp.concatenate([jnp.tile(att_g_q[j] * (hd ** -0.5 * LOG2E), ATT_HEADS),
                                  jnp.tile(att_g_k[j], ATT_HEADS),
                                  jnp.ones((d,), F32)]).reshape(1, -1)
            qkv = _norm_matmul(xf, g_mix, att_w_in_bf, j, 3 * d, gh=gh, hd=hd, n_norm=2 * d)
            mix = _attn_core(qkv, att_rel_bias[j], batch, seq)
            w_out = att_w_out_bf
        xf = _matmul_res(mix, w_out, j, xf)
        xf = _mlp(xf, norm_mlp_g[i].reshape(1, d), mlp_w_up_bf, mlp_w_down_bf, i)
    return xf.reshape(batch, seq, d)
```

```python
import functools

import jax
import jax.numpy as jnp
from jax import lax
from jax.experimental import pallas as pl
from jax.experimental.pallas import tpu as pltpu

F32 = jnp.float32
BF16 = jnp.bfloat16

EPS = 1e-6
CHUNK = 64
N_MIXERS = 2
GLA_HEADS = 4
GLA_GATE_RANK = 16
GLA_TAU = 16.0
ATT_HEADS = 16
LEFT_CHUNKS = 8
REL_CLIP = 256

LANES = 128
NORM_ROWS = 16
NORM_UNROLL = 8
HEADNORM_SPLIT = 4
MLP_SPLIT = 2
VMEM_LIMIT = 56 * 1024 * 1024
NEG = -1e30
LOG2E = 1.4426950408889634

GLA_ROWS = 256
GLA_HB = 4
ATT_QB = 256
ATT_KB = ATT_QB + LEFT_CHUNKS * CHUNK
ATT_HB = 16
ATT_INTERLEAVE = 1


def _params(*sem):
    return pltpu.CompilerParams(dimension_semantics=sem, vmem_limit_bytes=VMEM_LIMIT)


def _interleave(chains, width):
    for start in range(0, len(chains), width):
        group = chains[start:start + width]
        while group:
            alive = []
            for chain in group:
                try:
                    next(chain)
                    alive.append(chain)
                except StopIteration:
                    pass
            group = alive


def _rmsnorm_rows_to(x_ref, g_ref, dst_ref, copy_ref=None):
    rows = x_ref.shape[0]

    def body(r, carry):
        sl = pl.ds(pl.multiple_of(r * NORM_ROWS, NORM_ROWS), NORM_ROWS)
        x = x_ref[sl, :]
        ms = jnp.mean(x * x, axis=-1, keepdims=True)
        dst_ref[sl, :] = ((x * lax.rsqrt(ms + EPS)) * g_ref[...]).astype(dst_ref.dtype)
        if copy_ref is not None:
            copy_ref[sl, :] = x
        return carry

    lax.fori_loop(0, rows // NORM_ROWS, body, 0, unroll=NORM_UNROLL)


def _norm_matmul_headnorm_kernel(x_ref, g_ref, w_ref, gh_ref, o_ref, xn_ref, *, norm_blocks, hd):
    j = pl.program_id(1)

    @pl.when(j == 0)
    def _():
        _rmsnorm_rows_to(x_ref, g_ref, xn_ref)

    normed = j < norm_blocks
    rows = xn_ref.shape[0] // HEADNORM_SPLIT
    for part in range(HEADNORM_SPLIT):
        rs = slice(part * rows, (part + 1) * rows)
        acc = jnp.dot(xn_ref[rs, :], w_ref[...], preferred_element_type=F32)
        for h in range(acc.shape[1] // hd):
            cs = slice(h * hd, (h + 1) * hd)
            a = acc[:, cs]
            ms = jnp.mean(a * a, axis=-1, keepdims=True)
            scale = jnp.where(normed, lax.rsqrt(ms + EPS), 1.0)
            o_ref[rs, cs] = ((a * scale) * gh_ref[:, cs]).astype(o_ref.dtype)


def _norm_matmul_z_kernel(x_ref, g_ref, w_ref, wz_ref, o_ref, z_ref, xn_ref):
    @pl.when(pl.program_id(1) == 0)
    def _():
        _rmsnorm_rows_to(x_ref, g_ref, xn_ref)
        z_ref[...] = jnp.dot(xn_ref[...], wz_ref[...], preferred_element_type=F32)

    rows = xn_ref.shape[0] // HEADNORM_SPLIT
    for part in range(HEADNORM_SPLIT):
        rs = slice(part * rows, (part + 1) * rows)
        o_ref[rs, :] = jnp.dot(xn_ref[rs, :], w_ref[...], preferred_element_type=F32).astype(o_ref.dtype)


def _norm_matmul(x, g, w, layer, n, wz=None, gh=None, hd=None, n_norm=0, *, tm=1024, tn=2048):
    t, d = x.shape
    tm, tn = min(tm, t), min(tn, n)
    grid = (t // tm, n // tn)
    x_spec = pl.BlockSpec((tm, d), lambda i, j: (i, 0))
    g_spec = pl.BlockSpec((1, d), lambda i, j: (0, 0))
    w_spec = pl.BlockSpec((None, d, tn), lambda i, j: (layer, 0, j))
    o_spec = pl.BlockSpec((tm, tn), lambda i, j: (i, j))
    scratch = [pltpu.VMEM((tm, d), BF16)]
    if gh is not None:
        assert n_norm % tn == 0 and tn % hd == 0 and gh.shape[1] == n
        gh_spec = pl.BlockSpec((1, tn), lambda i, j: (0, j))
        return pl.pallas_call(
            functools.partial(_norm_matmul_headnorm_kernel, norm_blocks=n_norm // tn, hd=hd),
            out_shape=jax.ShapeDtypeStruct((t, n), BF16),
            grid=grid, in_specs=[x_spec, g_spec, w_spec, gh_spec], out_specs=o_spec,
            scratch_shapes=scratch, compiler_params=_params("parallel", "arbitrary"),
            name="norm_matmul_headnorm",
        )(x, g, w, gh)
    nz = wz.shape[1]
    wz_spec = pl.BlockSpec((d, nz), lambda i, j: (0, 0))
    z_spec = pl.BlockSpec((tm, nz), lambda i, j: (i, 0))
    return pl.pallas_call(
        _norm_matmul_z_kernel,
        out_shape=(jax.ShapeDtypeStruct((t, n), BF16), jax.ShapeDtypeStruct((t, nz), F32)),
        grid=grid, in_specs=[x_spec, g_spec, w_spec, wz_spec], out_specs=(o_spec, z_spec),
        scratch_shapes=scratch, compiler_params=_params("parallel", "arbitrary"),
        name="norm_matmul_z",
    )(x, g, w, wz)


def _matmul_res_kernel(a_ref, w_ref, r_ref, o_ref):
    o_ref[...] = r_ref[...] + jnp.dot(a_ref[...], w_ref[...], preferred_element_type=F32)


def _matmul_res(a, w, layer, res, *, tm=512, tn=2048):
    t, k = a.shape
    n = w.shape[2]
    tm, tn = min(tm, t), min(tn, n)
    return pl.pallas_call(
        _matmul_res_kernel,
        out_shape=jax.ShapeDtypeStruct((t, n), F32),
        grid=(t // tm, n // tn),
        in_specs=[pl.BlockSpec((tm, k), lambda i, j: (i, 0)),
                  pl.BlockSpec((None, k, tn), lambda i, j: (layer, 0, j)),
                  pl.BlockSpec((tm, tn), lambda i, j: (i, j))],
        out_specs=pl.BlockSpec((tm, tn), lambda i, j: (i, j)),
        compiler_params=_params("parallel", "parallel"),
        name="matmul_res",
    )(a, w, res)


def _mlp_kernel(x_ref, g_ref, wu_ref, wd_ref, o_ref, xn_ref):
    @pl.when(pl.program_id(1) == 0)
    def _():
        _rmsnorm_rows_to(x_ref, g_ref, xn_ref, copy_ref=o_ref)

    rows = xn_ref.shape[0] // MLP_SPLIT
    for part in range(MLP_SPLIT):
        rs = slice(part * rows, (part + 1) * rows)
        h = jnp.dot(xn_ref[rs, :], wu_ref[...], preferred_element_type=F32)
        h = jnp.square(jnp.maximum(h, 0.0)).astype(BF16)
        o_ref[rs, :] += jnp.dot(h, wd_ref[...], preferred_element_type=F32)


def _mlp(x, g, w_up, w_down, layer, *, tm=512, tf=2048):
    t, d = x.shape
    f = w_up.shape[2]
    tm, tf = min(tm, t), min(tf, f)
    return pl.pallas_call(
        _mlp_kernel,
        out_shape=jax.ShapeDtypeStruct((t, d), F32),
        grid=(t // tm, f // tf),
        in_specs=[pl.BlockSpec((tm, d), lambda i, j: (i, 0)),
                  pl.BlockSpec((1, d), lambda i, j: (0, 0)),
                  pl.BlockSpec((None, d, tf), lambda i, j: (layer, 0, j)),
                  pl.BlockSpec((None, tf, d), lambda i, j: (layer, j, 0))],
        out_specs=pl.BlockSpec((tm, d), lambda i, j: (i, 0)),
        scratch_shapes=[pltpu.VMEM((tm, d), BF16)],
        compiler_params=_params("parallel", "arbitrary"),
        name="mlp",
    )(x, g, w_up, w_down)


def _split_bf16(a):
    hi = a.astype(BF16)
    lo = (a - hi.astype(F32)).astype(BF16)
    return hi, lo


def _dot(a, b):
    return jnp.dot(a, b, preferred_element_type=F32)


def _dot_tn(a, b):
    return lax.dot_general(a, b, (((0,), (0,)), ((), ())), preferred_element_type=F32)


def _dot_nt(a, b):
    return lax.dot_general(a, b, (((1,), (1,)), ((), ())), preferred_element_type=F32)


def _gla_kernel(q_ref, k_ref, v_ref, r_ref, z_ref, wgh_ref, wgl_ref, bg_ref, go_ref, tri_ref,
                o_ref, state_ref):
    hb = state_ref.shape[0]

    @pl.when(pl.program_id(2) == 0)
    def _():
        state_ref[...] = jnp.zeros_like(state_ref)

    z_parts = _split_bf16(z_ref[...])
    tri = tri_ref[...]
    _interleave([_gla_head(h, q_ref, k_ref, v_ref, r_ref, z_parts, wgh_ref, wgl_ref, bg_ref, go_ref,
                           tri, o_ref, state_ref) for h in range(hb)], hb)


def _gla_head(h, q_ref, k_ref, v_ref, r_ref, z_parts, wgh_ref, wgl_ref, bg_ref, go_ref, tri,
              o_ref, state_ref):
    rows = q_ref.shape[0]
    hb = state_ref.shape[0]
    dk = q_ref.shape[1] // hb
    dv = v_ref.shape[1] // hb
    ks = slice(h * dk, (h + 1) * dk)
    vs = slice(h * dv, (h + 1) * dv)
    nc = rows // CHUNK
    chunk = lambda c: slice(c * CHUNK, (c + 1) * CHUNK)

    z_hi, z_lo = z_parts
    wg_hi = wgh_ref[:, ks]
    logit = _dot(z_hi, wg_hi) + _dot(z_lo, wg_hi) + _dot(z_hi, wgl_ref[:, ks]) + bg_ref[:, ks]
    t = logit * LOG2E
    yield
    log_a = (jnp.minimum(t, 0.0) - jnp.log2(1.0 + jnp.exp2(-jnp.abs(t)))) * (1.0 / GLA_TAU)

    la_hi, la_lo = _split_bf16(log_a)
    b = _dot(tri, la_hi) + _dot(tri, la_lo)
    yield
    bl = [b[(c + 1) * CHUNK - 1:(c + 1) * CHUNK, :] for c in range(nc)]

    def dec(c1, c2):
        return jnp.exp2(functools.reduce(jnp.add, bl[c1:c2]))

    q_dec = q_ref[:, ks].astype(F32) * (dk ** -0.5) * jnp.exp2(b)
    k = k_ref[:, ks].astype(F32)
    k_inv = (k * jnp.exp2(-b)).astype(BF16)
    k_end = [k[chunk(c)] * jnp.exp2(bl[c] - b[chunk(c)]) for c in range(nc)]
    q_dec_bf = q_dec.astype(BF16)
    yield

    trow = lax.broadcasted_iota(jnp.int32, (CHUNK, rows), 0)
    scol = lax.broadcasted_iota(jnp.int32, (CHUNK, rows), 1)
    a_rows = []
    for c in range(nc):
        keys = [(k_end[cp] * dec(cp + 1, c) if cp + 1 < c else k_end[cp]).astype(BF16)
                for cp in range(c)]
        keys.append(k_inv[chunk(c)])
        if c + 1 < nc:
            keys.append(jnp.zeros(((nc - 1 - c) * CHUNK, dk), BF16))
        s = _dot_nt(q_dec_bf[chunk(c)], jnp.concatenate(keys, axis=0))
        a_rows.append(jnp.where(scol <= trow + c * CHUNK, s, 0.0).astype(BF16))
    a = jnp.concatenate(a_rows, axis=0)
    yield

    q_in = jnp.concatenate(
        [(q_dec[chunk(c)] * dec(0, c)).astype(BF16) if c else q_dec_bf[chunk(0)] for c in range(nc)],
        axis=0)
    k_out = jnp.concatenate(
        [(k_end[c] * dec(c + 1, nc) if c + 1 < nc else k_end[c]).astype(BF16) for c in range(nc)],
        axis=0)

    v = v_ref[:, vs]
    state = state_ref[h]
    o = _dot(a, v) + _dot_nt(q_in, state.astype(BF16))
    state_ref[h] = state * dec(0, nc) + _dot_tn(v, k_out)
    yield

    for c in range(nc):
        oc = o[chunk(c)]
        ms = jnp.mean(oc * oc, axis=-1, keepdims=True)
        r = r_ref[chunk(c), vs].astype(F32)
        y = (oc * lax.rsqrt(ms + EPS)) * go_ref[:, vs]
        o_ref[chunk(c), vs] = (y * (r * (jnp.tanh(0.5 * r) + 1.0))).astype(o_ref.dtype)


def _gla_core(proj, z, wg, bg, g_half, batch, seq):
    t = proj.shape[0]
    d = g_half.shape[1]
    dv = d // GLA_HEADS
    dk = dv // 2
    rows = min(GLA_ROWS, seq)
    nblk = seq // rows
    hb = GLA_HB
    ng = GLA_HEADS // hb
    wk, wv = hb * dk, hb * dv
    k_off = ng
    v_off = (2 * GLA_HEADS * dk) // wv
    r_off = v_off + ng
    assert (2 * GLA_HEADS * dk) % wv == 0
    wg_hi, wg_lo = _split_bf16(wg)
    idx = jnp.arange(rows)
    tri = ((idx[:, None] // CHUNK == idx[None, :] // CHUNK) & (idx[None, :] <= idx[:, None])).astype(BF16)
    rowmap = lambda b, h, n: b * nblk + n
    wg_spec = pl.BlockSpec((wg.shape[0], wk), lambda b, h, n: (0, h))
    return pl.pallas_call(
        _gla_kernel,
        out_shape=jax.ShapeDtypeStruct((t, d), BF16),
        grid=(batch, ng, nblk),
        in_specs=[pl.BlockSpec((rows, wk), lambda b, h, n: (rowmap(b, h, n), h)),
                  pl.BlockSpec((rows, wk), lambda b, h, n: (rowmap(b, h, n), k_off + h)),
                  pl.BlockSpec((rows, wv), lambda b, h, n: (rowmap(b, h, n), v_off + h)),
                  pl.BlockSpec((rows, wv), lambda b, h, n: (rowmap(b, h, n), r_off + h)),
                  pl.BlockSpec((rows, z.shape[1]), lambda b, h, n: (rowmap(b, h, n), 0)),
                  wg_spec, wg_spec,
                  pl.BlockSpec((1, wk), lambda b, h, n: (0, h)),
                  pl.BlockSpec((1, wv), lambda b, h, n: (0, h)),
                  pl.BlockSpec((rows, rows), lambda b, h, n: (0, 0))],
        out_specs=pl.BlockSpec((rows, wv), lambda b, h, n: (rowmap(b, h, n), h)),
        scratch_shapes=[pltpu.VMEM((hb, dv, dk), F32)],
        compiler_params=_params("parallel", "parallel", "arbitrary"),
        name="gla_core",
    )(proj, proj, proj, proj, z, wg_hi, wg_lo, bg, g_half, tri)


def _attn_kernel(q_ref, k0_ref, k1_ref, k2_ref, v0_ref, v1_ref, v2_ref, gen_ref, o_ref, bias_ref):
    qb = q_ref.shape[0]
    hb = bias_ref.shape[0]
    hd = q_ref.shape[1] // hb
    nkb = ATT_KB // qb
    cpb = qb // CHUNK
    i = pl.program_id(2)
    k_refs = (k0_ref, k1_ref, k2_ref)
    v_refs = (v0_ref, v1_ref, v2_ref)

    @pl.when((pl.program_id(1) == 0) & (i == 0))
    def _():
        cq = lax.broadcasted_iota(jnp.int32, (qb, qb), 0) // CHUNK
        ck = lax.broadcasted_iota(jnp.int32, (qb, qb), 1) // CHUNK
        for j in range(nkb):
            band = (ck + cpb * j >= cq) & (ck + cpb * j <= cq + LEFT_CHUNKS)
            for h in range(hb):
                gen = jnp.broadcast_to(gen_ref[h, j:j + 1, :], (qb, 2 * qb))
                toe = pltpu.roll(gen, qb + 1, 1, stride=1, stride_axis=0)[:, :qb]
                bias_ref[h, :, j * qb:(j + 1) * qb] = jnp.where(band, toe, NEG)

    ones = jnp.ones((qb, hd), BF16)
    q_ext = jnp.full((qb, hd), 1.0 / hd, BF16)
    k_ext = jnp.concatenate(
        [jnp.full((qb, hd), jnp.where(i >= nkb - 1 - j, 0.0, NEG), F32).astype(BF16)
         for j in range(nkb)], axis=0)
    def head(h):
        cs = slice(h * hd, (h + 1) * hd)
        q = jnp.concatenate([q_ref[:, cs], q_ext], axis=1)
        k = jnp.concatenate([jnp.concatenate([r[:, cs] for r in k_refs], axis=0), k_ext], axis=1)
        s = _dot_nt(q, k) + bias_ref[h]
        yield
        m = jnp.max(s, axis=-1, keepdims=True)
        yield
        p = jnp.exp2(s - m).astype(BF16)
        yield
        v1 = jnp.concatenate([jnp.concatenate([r[:, cs] for r in v_refs], axis=0),
                              jnp.concatenate([ones] * nkb, axis=0)], axis=1)
        acc = _dot(p, v1)
        yield
        o_ref[:, cs] = (acc[:, :hd] / acc[:, hd:]).astype(o_ref.dtype)

    _interleave([head(h) for h in range(hb)], ATT_INTERLEAVE)


def _attn_bias_generators(rel_bias, qb):
    left = LEFT_CHUNKS * CHUNK
    x = jnp.arange(2 * qb)[None, :]
    j = jnp.arange((qb + left) // qb)[:, None]
    dist = left - qb * j + (qb - 1) - x
    idx = jnp.clip(dist, -(CHUNK - 1), REL_CLIP) + (CHUNK - 1)
    return rel_bias[:, idx].astype(F32) * LOG2E


def _attn_core(qkv, rel_bias, batch, seq):
    t = qkv.shape[0]
    d = qkv.shape[1] // 3
    hd = d // ATT_HEADS
    qb = ATT_QB
    assert seq % qb == 0 and ATT_KB % qb == 0 and ATT_KB // qb == 3
    nq = seq // qb
    hb = ATT_HB
    ng = ATT_HEADS // hb
    gen = _attn_bias_generators(rel_bias, qb)
    w = hb * hd

    def kv_spec(off, back):
        return pl.BlockSpec((qb, w), lambda g, b, i: (b * nq + jnp.maximum(i - back, 0), off * ng + g))

    return pl.pallas_call(
        _attn_kernel,
        out_shape=jax.ShapeDtypeStruct((t, d), BF16),
        grid=(ng, batch, nq),
        in_specs=[pl.BlockSpec((qb, w), lambda g, b, i: (b * nq + i, g)),
                  kv_spec(1, 2), kv_spec(1, 1), kv_spec(1, 0),
                  kv_spec(2, 2), kv_spec(2, 1), kv_spec(2, 0),
                  pl.BlockSpec((hb,) + gen.shape[1:], lambda g, b, i: (g, 0, 0))],
        out_specs=pl.BlockSpec((qb, w), lambda g, b, i: (b * nq + i, g)),
        scratch_shapes=[pltpu.VMEM((hb, qb, ATT_KB), F32)],
        compiler_params=_params("arbitrary", "arbitrary", "arbitrary"),
        name="attn_core",
    )(qkv, qkv, qkv, qkv, qkv, qkv, qkv, gen)


def kernel(x, norm_mix_g, norm_mlp_g, gla_w_in, gla_w_gate_up, gla_b_gate, gla_g_out, gla_w_out,
           att_w_in, att_g_q, att_g_k, att_rel_bias, att_w_out, mlp_w_up, mlp_w_down):
    batch, seq, d = x.shape
    depth = norm_mix_g.shape[0]
    xf = x.reshape(batch * seq, d)
    gla_w_in_bf, gla_w_out_bf = gla_w_in.astype(BF16), gla_w_out.astype(BF16)
    att_w_in_bf, att_w_out_bf = att_w_in.astype(BF16), att_w_out.astype(BF16)
    mlp_w_up_bf, mlp_w_down_bf = mlp_w_up.astype(BF16), mlp_w_down.astype(BF16)
    for i in range(depth):
        j = i // N_MIXERS
        g_mix = norm_mix_g[i].reshape(1, d)
        if i % N_MIXERS == 0:
            n_main = gla_w_in.shape[2] - GLA_GATE_RANK
            w_z = jnp.pad(gla_w_in[j, :, n_main:], ((0, 0), (0, LANES - GLA_GATE_RANK))).astype(BF16)
            wg = jnp.pad(gla_w_gate_up[j], ((0, LANES - GLA_GATE_RANK), (0, 0)))
            proj, z = _norm_matmul(xf, g_mix, gla_w_in_bf, j, n_main, wz=w_z)
            mix = _gla_core(proj, z, wg, gla_b_gate[j].reshape(1, -1),
                            0.5 * gla_g_out[j].reshape(1, d), batch, seq)
            w_out = gla_w_out_bf
        else:
            hd = att_g_q.shape[1]
            gh = jnp.concatenate([jnp.tile(att_g_q[j] * (hd ** -0.5 * LOG2E), ATT_HEADS),
                                  jnp.tile(att_g_k[j], ATT_HEADS),
                                  jnp.ones((d,), F32)]).reshape(1, -1)
            qkv = _norm_matmul(xf, g_mix, att_w_in_bf, j, 3 * d, gh=gh, hd=hd, n_norm=2 * d)
            mix = _attn_core(qkv, att_rel_bias[j], batch, seq)
            w_out = att_w_out_bf
        xf = _matmul_res(mix, w_out, j, xf)
        xf = _mlp(xf, norm_mlp_g[i].reshape(1, d), mlp_w_up_bf, mlp_w_down_bf, i)
    return xf.reshape(batch, seq, d)
```

```python
import functools

import jax
import jax.numpy as jnp
from jax import lax
from jax.experimental import pallas as pl
from jax.experimental.pallas import tpu as pltpu

F32 = jnp.float32
BF16 = jnp.bfloat16

EPS = 1e-6
CHUNK = 64
N_MIXERS = 2
GLA_HEADS = 4
GLA_GATE_RANK = 16
GLA_TAU = 16.0
ATT_HEADS = 16
LEFT_CHUNKS = 8
REL_CLIP = 256

LANES = 128
NORM_ROWS = 16
NORM_UNROLL = 8
HEADNORM_SPLIT = 4
MLP_SPLIT = 2
VMEM_LIMIT = 56 * 1024 * 1024
NEG = -1e30
LOG2E = 1.4426950408889634

GLA_ROWS = 256
GLA_HB = 4
ATT_QB = 256
ATT_KB = ATT_QB + LEFT_CHUNKS * CHUNK
ATT_HB = 16
ATT_INTERLEAVE = 1


def _params(*sem):
    return pltpu.CompilerParams(dimension_semantics=sem, vmem_limit_bytes=VMEM_LIMIT)


def _interleave(chains, width):
    for start in range(0, len(chains), width):
        group = chains[start:start + width]
        while group:
            alive = []
            for chain in group:
                try:
                    next(chain)
                    alive.append(chain)
                except StopIteration:
                    pass
            group = alive


def _rmsnorm_rows_to(x_ref, g_ref, dst_ref, copy_ref=None):
    rows = x_ref.shape[0]

    def body(r, carry):
        sl = pl.ds(pl.multiple_of(r * NORM_ROWS, NORM_ROWS), NORM_ROWS)
        x = x_ref[sl, :]
        ms = jnp.mean(x * x, axis=-1, keepdims=True)
        dst_ref[sl, :] = ((x * lax.rsqrt(ms + EPS)) * g_ref[...]).astype(dst_ref.dtype)
        if copy_ref is not None:
            copy_ref[sl, :] = x
        return carry

    lax.fori_loop(0, rows // NORM_ROWS, body, 0, unroll=NORM_UNROLL)


def _rmsnorm_rows_static(x_ref, g_ref, dst_ref, rs, copy_ref=None):
    for r0 in range(rs.start, rs.stop, NORM_ROWS):
        sl = slice(r0, r0 + NORM_ROWS)
        x = x_ref[sl, :]
        ms = jnp.mean(x * x, axis=-1, keepdims=True)
        dst_ref[sl, :] = ((x * lax.rsqrt(ms + EPS)) * g_ref[...]).astype(dst_ref.dtype)
        if copy_ref is not None:
            copy_ref[sl, :] = x


def _norm_matmul_headnorm_kernel(x_ref, g_ref, w_ref, gh_ref, o_ref, xn_ref, *, norm_blocks, hd):
    j = pl.program_id(1)
    normed = j < norm_blocks
    rows = xn_ref.shape[0] // HEADNORM_SPLIT

    def step(first):
        for part in range(HEADNORM_SPLIT):
            rs = slice(part * rows, (part + 1) * rows)
            if first:
                _rmsnorm_rows_static(x_ref, g_ref, xn_ref, rs)
            acc = jnp.dot(xn_ref[rs, :], w_ref[...], preferred_element_type=F32)
            for h in range(acc.shape[1] // hd):
                cs = slice(h * hd, (h + 1) * hd)
                a = acc[:, cs]
                ms = jnp.mean(a * a, axis=-1, keepdims=True)
                scale = jnp.where(normed, lax.rsqrt(ms + EPS), 1.0)
                o_ref[rs, cs] = ((a * scale) * gh_ref[:, cs]).astype(o_ref.dtype)

    pl.when(j == 0)(functools.partial(step, True))
    pl.when(j != 0)(functools.partial(step, False))


def _norm_matmul_z_kernel(x_ref, g_ref, w_ref, wz_ref, o_ref, z_ref, xn_ref):
    rows = xn_ref.shape[0] // HEADNORM_SPLIT

    def step(first):
        for part in range(HEADNORM_SPLIT):
            rs = slice(part * rows, (part + 1) * rows)
            if first:
                _rmsnorm_rows_static(x_ref, g_ref, xn_ref, rs)
            o_ref[rs, :] = jnp.dot(xn_ref[rs, :], w_ref[...],
                                   preferred_element_type=F32).astype(o_ref.dtype)
        if first:
            z_ref[...] = jnp.dot(xn_ref[...], wz_ref[...], preferred_element_type=F32)

    j = pl.program_id(1)
    pl.when(j == 0)(functools.partial(step, True))
    pl.when(j != 0)(functools.partial(step, False))


def _norm_matmul(x, g, w, layer, n, wz=None, gh=None, hd=None, n_norm=0, *, tm=1024, tn=2048):
    t, d = x.shape
    tm, tn = min(tm, t), min(tn, n)
    grid = (t // tm, n // tn)
    x_spec = pl.BlockSpec((tm, d), lambda i, j: (i, 0))
    g_spec = pl.BlockSpec((1, d), lambda i, j: (0, 0))
    w_spec = pl.BlockSpec((None, d, tn), lambda i, j: (layer, 0, j))
    o_spec = pl.BlockSpec((tm, tn), lambda i, j: (i, j))
    scratch = [pltpu.VMEM((tm, d), BF16)]
    if gh is not None:
        assert n_norm % tn == 0 and tn % hd == 0 and gh.shape[1] == n
        gh_spec = pl.BlockSpec((1, tn), lambda i, j: (0, j))
        return pl.pallas_call(
            functools.partial(_norm_matmul_headnorm_kernel, norm_blocks=n_norm // tn, hd=hd),
            out_shape=jax.ShapeDtypeStruct((t, n), BF16),
            grid=grid, in_specs=[x_spec, g_spec, w_spec, gh_spec], out_specs=o_spec,
            scratch_shapes=scratch, compiler_params=_params("parallel", "arbitrary"),
            name="norm_matmul_headnorm",
        )(x, g, w, gh)
    nz = wz.shape[1]
    wz_spec = pl.BlockSpec((d, nz), lambda i, j: (0, 0))
    z_spec = pl.BlockSpec((tm, nz), lambda i, j: (i, 0))
    return pl.pallas_call(
        _norm_matmul_z_kernel,
        out_shape=(jax.ShapeDtypeStruct((t, n), BF16), jax.ShapeDtypeStruct((t, nz), F32)),
        grid=grid, in_specs=[x_spec, g_spec, w_spec, wz_spec], out_specs=(o_spec, z_spec),
        scratch_shapes=scratch, compiler_params=_params("parallel", "arbitrary"),
        name="norm_matmul_z",
    )(x, g, w, wz)


def _matmul_res_kernel(a_ref, w_ref, r_ref, o_ref):
    o_ref[...] = r_ref[...] + jnp.dot(a_ref[...], w_ref[...], preferred_element_type=F32)


def _matmul_res(a, w, layer, res, *, tm=512, tn=2048):
    t, k = a.shape
    n = w.shape[2]
    tm, tn = min(tm, t), min(tn, n)
    return pl.pallas_call(
        _matmul_res_kernel,
        out_shape=jax.ShapeDtypeStruct((t, n), F32),
        grid=(t // tm, n // tn),
        in_specs=[pl.BlockSpec((tm, k), lambda i, j: (i, 0)),
                  pl.BlockSpec((None, k, tn), lambda i, j: (layer, 0, j)),
                  pl.BlockSpec((tm, tn), lambda i, j: (i, j))],
        out_specs=pl.BlockSpec((tm, tn), lambda i, j: (i, j)),
        compiler_params=_params("parallel", "parallel"),
        name="matmul_res",
    )(a, w, res)


def _mlp_kernel(x_ref, g_ref, wu_ref, wd_ref, o_ref, xn_ref):
    rows = xn_ref.shape[0] // MLP_SPLIT

    def step(first):
        for part in range(MLP_SPLIT):
            rs = slice(part * rows, (part + 1) * rows)
            if first:
                _rmsnorm_rows_static(x_ref, g_ref, xn_ref, rs)
            h = jnp.dot(xn_ref[rs, :], wu_ref[...], preferred_element_type=F32)
            h = jnp.square(jnp.maximum(h, 0.0)).astype(BF16)
            y = jnp.dot(h, wd_ref[...], preferred_element_type=F32)
            o_ref[rs, :] = (x_ref[rs, :] if first else o_ref[rs, :]) + y

    f = pl.program_id(1)
    pl.when(f == 0)(functools.partial(step, True))
    pl.when(f != 0)(functools.partial(step, False))


def _mlp(x, g, w_up, w_down, layer, *, tm=512, tf=2048):
    t, d = x.shape
    f = w_up.shape[2]
    tm, tf = min(tm, t), min(tf, f)
    return pl.pallas_call(
        _mlp_kernel,
        out_shape=jax.ShapeDtypeStruct((t, d), F32),
        grid=(t // tm, f // tf),
        in_specs=[pl.BlockSpec((tm, d), lambda i, j: (i, 0)),
                  pl.BlockSpec((1, d), lambda i, j: (0, 0)),
                  pl.BlockSpec((None, d, tf), lambda i, j: (layer, 0, j)),
                  pl.BlockSpec((None, tf, d), lambda i, j: (layer, j, 0))],
        out_specs=pl.BlockSpec((tm, d), lambda i, j: (i, 0)),
        scratch_shapes=[pltpu.VMEM((tm, d), BF16)],
        compiler_params=_params("parallel", "arbitrary"),
        name="mlp",
    )(x, g, w_up, w_down)


def _split_bf16(a):
    hi = a.astype(BF16)
    lo = (a - hi.astype(F32)).astype(BF16)
    return hi, lo


def _dot(a, b):
    return jnp.dot(a, b, preferred_element_type=F32)


def _dot_tn(a, b):
    return lax.dot_general(a, b, (((0,), (0,)), ((), ())), preferred_element_type=F32)


def _dot_nt(a, b):
    return lax.dot_general(a, b, (((1,), (1,)), ((), ())), preferred_element_type=F32)


def _gla_kernel(q_ref, k_ref, v_ref, r_ref, z_ref, wgh_ref, wgl_ref, bg_ref, go_ref, tri_ref,
                o_ref, state_ref):
    hb = state_ref.shape[0]

    @pl.when(pl.program_id(2) == 0)
    def _():
        state_ref[...] = jnp.zeros_like(state_ref)

    z_parts = _split_bf16(z_ref[...])
    tri = tri_ref[...]
    _interleave([_gla_head(h, q_ref, k_ref, v_ref, r_ref, z_parts, wgh_ref, wgl_ref, bg_ref, go_ref,
                           tri, o_ref, state_ref) for h in range(hb)], hb)


def _gla_head(h, q_ref, k_ref, v_ref, r_ref, z_parts, wgh_ref, wgl_ref, bg_ref, go_ref, tri,
              o_ref, state_ref):
    rows = q_ref.shape[0]
    hb = state_ref.shape[0]
    dk = q_ref.shape[1] // hb
    dv = v_ref.shape[1] // hb
    ks = slice(h * dk, (h + 1) * dk)
    vs = slice(h * dv, (h + 1) * dv)
    nc = rows // CHUNK
    chunk = lambda c: slice(c * CHUNK, (c + 1) * CHUNK)

    z_hi, z_lo = z_parts
    wg_hi = wgh_ref[:, ks]
    logit = _dot(z_hi, wg_hi) + _dot(z_lo, wg_hi) + _dot(z_hi, wgl_ref[:, ks]) + bg_ref[:, ks]
    t = logit * LOG2E
    yield
    log_a = (jnp.minimum(t, 0.0) - jnp.log2(1.0 + jnp.exp2(-jnp.abs(t)))) * (1.0 / GLA_TAU)

    la_hi, la_lo = _split_bf16(log_a)
    b = _dot(tri, la_hi) + _dot(tri, la_lo)
    yield
    bl = [b[(c + 1) * CHUNK - 1:(c + 1) * CHUNK, :] for c in range(nc)]

    def dec(c1, c2):
        return jnp.exp2(functools.reduce(jnp.add, bl[c1:c2]))

    q_dec = q_ref[:, ks].astype(F32) * (dk ** -0.5) * jnp.exp2(b)
    k = k_ref[:, ks].astype(F32)
    k_inv = (k * jnp.exp2(-b)).astype(BF16)
    k_end = [k[chunk(c)] * jnp.exp2(bl[c] - b[chunk(c)]) for c in range(nc)]
    q_dec_bf = q_dec.astype(BF16)
    yield

    trow = lax.broadcasted_iota(jnp.int32, (CHUNK, rows), 0)
    scol = lax.broadcasted_iota(jnp.int32, (CHUNK, rows), 1)
    a_rows = []
    for c in range(nc):
        keys = [(k_end[cp] * dec(cp + 1, c) if cp + 1 < c else k_end[cp]).astype(BF16)
                for cp in range(c)]
        keys.append(k_inv[chunk(c)])
        if c + 1 < nc:
            keys.append(jnp.zeros(((nc - 1 - c) * CHUNK, dk), BF16))
        s = _dot_nt(q_dec_bf[chunk(c)], jnp.concatenate(keys, axis=0))
        a_rows.append(jnp.where(scol <= trow + c * CHUNK, s, 0.0).astype(BF16))
    a = jnp.concatenate(a_rows, axis=0)
    yield

    q_in = jnp.concatenate(
        [(q_dec[chunk(c)] * dec(0, c)).astype(BF16) if c else q_dec_bf[chunk(0)] for c in range(nc)],
        axis=0)
    k_out = jnp.concatenate(
        [(k_end[c] * dec(c + 1, nc) if c + 1 < nc else k_end[c]).astype(BF16) for c in range(nc)],
        axis=0)

    v = v_ref[:, vs]
    state = state_ref[h]
    o = _dot(a, v) + _dot_nt(q_in, state.astype(BF16))
    state_ref[h] = state * dec(0, nc) + _dot_tn(v, k_out)
    yield

    for c in range(nc):
        oc = o[chunk(c)]
        ms = jnp.mean(oc * oc, axis=-1, keepdims=True)
        r = r_ref[chunk(c), vs].astype(F32)
        y = (oc * lax.rsqrt(ms + EPS)) * go_ref[:, vs]
        o_ref[chunk(c), vs] = (y * (r * (jnp.tanh(0.5 * r) + 1.0))).astype(o_ref.dtype)


def _gla_core(proj, z, wg, bg, g_half, batch, seq):
    t = proj.shape[0]
    d = g_half.shape[1]
    dv = d // GLA_HEADS
    dk = dv // 2
    rows = min(GLA_ROWS, seq)
    nblk = seq // rows
    hb = GLA_HB
    ng = GLA_HEADS // hb
    wk, wv = hb * dk, hb * dv
    k_off = ng
    v_off = (2 * GLA_HEADS * dk) // wv
    r_off = v_off + ng
    assert (2 * GLA_HEADS * dk) % wv == 0
    wg_hi, wg_lo = _split_bf16(wg)
    idx = jnp.arange(rows)
    tri = ((idx[:, None] // CHUNK == idx[None, :] // CHUNK) & (idx[None, :] <= idx[:, None])).astype(BF16)
    rowmap = lambda b, h, n: b * nblk + n
    wg_spec = pl.BlockSpec((wg.shape[0], wk), lambda b, h, n: (0, h))
    return pl.pallas_call(
        _gla_kernel,
        out_shape=jax.ShapeDtypeStruct((t, d), BF16),
        grid=(batch, ng, nblk),
        in_specs=[pl.BlockSpec((rows, wk), lambda b, h, n: (rowmap(b, h, n), h)),
                  pl.BlockSpec((rows, wk), lambda b, h, n: (rowmap(b, h, n), k_off + h)),
                  pl.BlockSpec((rows, wv), lambda b, h, n: (rowmap(b, h, n), v_off + h)),
                  pl.BlockSpec((rows, wv), lambda b, h, n: (rowmap(b, h, n), r_off + h)),
                  pl.BlockSpec((rows, z.shape[1]), lambda b, h, n: (rowmap(b, h, n), 0)),
                  wg_spec, wg_spec,
                  pl.BlockSpec((1, wk), lambda b, h, n: (0, h)),
                  pl.BlockSpec((1, wv), lambda b, h, n: (0, h)),
                  pl.BlockSpec((rows, rows), lambda b, h, n: (0, 0))],
        out_specs=pl.BlockSpec((rows, wv), lambda b, h, n: (rowmap(b, h, n), h)),
        scratch_shapes=[pltpu.VMEM((hb, dv, dk), F32)],
        compiler_params=_params("parallel", "parallel", "arbitrary"),
        name="gla_core",
    )(proj, proj, proj, proj, z, wg_hi, wg_lo, bg, g_half, tri)


def _attn_kernel(q_ref, k0_ref, k1_ref, k2_ref, v0_ref, v1_ref, v2_ref, gen_ref, o_ref, bias_ref):
    qb = q_ref.shape[0]
    hb = bias_ref.shape[0]
    hd = q_ref.shape[1] // hb
    nkb = ATT_KB // qb
    cpb = qb // CHUNK
    i = pl.program_id(2)
    k_refs = (k0_ref, k1_ref, k2_ref)
    v_refs = (v0_ref, v1_ref, v2_ref)

    @pl.when((pl.program_id(1) == 0) & (i == 0))
    def _():
        cq = lax.broadcasted_iota(jnp.int32, (qb, qb), 0) // CHUNK
        ck = lax.broadcasted_iota(jnp.int32, (qb, qb), 1) // CHUNK
        for j in range(nkb):
            band = (ck + cpb * j >= cq) & (ck + cpb * j <= cq + LEFT_CHUNKS)
            for h in range(hb):
                gen = jnp.broadcast_to(gen_ref[h, j:j + 1, :], (qb, 2 * qb))
                toe = pltpu.roll(gen, qb + 1, 1, stride=1, stride_axis=0)[:, :qb]
                bias_ref[h, :, j * qb:(j + 1) * qb] = jnp.where(band, toe, NEG)

    ones = jnp.ones((qb, hd), BF16)
    q_ext = jnp.full((qb, hd), 1.0 / hd, BF16)
    k_ext = jnp.concatenate(
        [jnp.full((qb, hd), jnp.where(i >= nkb - 1 - j, 0.0, NEG), F32).astype(BF16)
         for j in range(nkb)], axis=0)
    def head(h):
        cs = slice(h * hd, (h + 1) * hd)
        q = jnp.concatenate([q_ref[:, cs], q_ext], axis=1)
        k = jnp.concatenate([jnp.concatenate([r[:, cs] for r in k_refs], axis=0), k_ext], axis=1)
        s = _dot_nt(q, k) + bias_ref[h]
        yield
        m = jnp.max(s, axis=-1, keepdims=True)
        yield
        p = jnp.exp2(s - m).astype(BF16)
        yield
        v1 = jnp.concatenate([jnp.concatenate([r[:, cs] for r in v_refs], axis=0),
                              jnp.concatenate([ones] * nkb, axis=0)], axis=1)
        acc = _dot(p, v1)
        yield
        o_ref[:, cs] = (acc[:, :hd] / acc[:, hd:]).astype(o_ref.dtype)

    _interleave([head(h) for h in range(hb)], ATT_INTERLEAVE)


def _attn_bias_generators(rel_bias, qb):
    left = LEFT_CHUNKS * CHUNK
    x = jnp.arange(2 * qb)[None, :]
    j = jnp.arange((qb + left) // qb)[:, None]
    dist = left - qb * j + (qb - 1) - x
    idx = jnp.clip(dist, -(CHUNK - 1), REL_CLIP) + (CHUNK - 1)
    return rel_bias[:, idx].astype(F32) * LOG2E


def _attn_core(qkv, rel_bias, batch, seq):
    t = qkv.shape[0]
    d = qkv.shape[1] // 3
    hd = d // ATT_HEADS
    qb = ATT_QB
    assert seq % qb == 0 and ATT_KB % qb == 0 and ATT_KB // qb == 3
    nq = seq // qb
    hb = ATT_HB
    ng = ATT_HEADS // hb
    gen = _attn_bias_generators(rel_bias, qb)
    w = hb * hd

    def kv_spec(off, back):
        return pl.BlockSpec((qb, w), lambda g, b, i: (b * nq + jnp.maximum(i - back, 0), off * ng + g))

    return pl.pallas_call(
        _attn_kernel,
        out_shape=jax.ShapeDtypeStruct((t, d), BF16),
        grid=(ng, batch, nq),
        in_specs=[pl.BlockSpec((qb, w), lambda g, b, i: (b * nq + i, g)),
                  kv_spec(1, 2), kv_spec(1, 1), kv_spec(1, 0),
                  kv_spec(2, 2), kv_spec(2, 1), kv_spec(2, 0),
                  pl.BlockSpec((hb,) + gen.shape[1:], lambda g, b, i: (g, 0, 0))],
        out_specs=pl.BlockSpec((qb, w), lambda g, b, i: (b * nq + i, g)),
        scratch_shapes=[pltpu.VMEM((hb, qb, ATT_KB), F32)],
        compiler_params=_params("arbitrary", "arbitrary", "arbitrary"),
        name="attn_core",
    )(qkv, qkv, qkv, qkv, qkv, qkv, qkv, gen)


def kernel(x, norm_mix_g, norm_mlp_g, gla_w_in, gla_w_gate_up, gla_b_gate, gla_g_out, gla_w_out,
           att_w_in, att_g_q, att_g_k, att_rel_bias, att_w_out, mlp_w_up, mlp_w_down):
    batch, seq, d = x.shape
    depth = norm_mix_g.shape[0]
    xf = x.reshape(batch * seq, d)
    gla_w_in_bf, gla_w_out_bf = gla_w_in.astype(BF16), gla_w_out.astype(BF16)
    att_w_in_bf, att_w_out_bf = att_w_in.astype(BF16), att_w_out.astype(BF16)
    mlp_w_up_bf, mlp_w_down_bf = mlp_w_up.astype(BF16), mlp_w_down.astype(BF16)
    for i in range(depth):
        j = i // N_MIXERS
        g_mix = norm_mix_g[i].reshape(1, d)
        if i % N_MIXERS == 0:
            n_main = gla_w_in.shape[2] - GLA_GATE_RANK
            w_z = jnp.pad(gla_w_in[j, :, n_main:], ((0, 0), (0, LANES - GLA_GATE_RANK))).astype(BF16)
            wg = jnp.pad(gla_w_gate_up[j], ((0, LANES - GLA_GATE_RANK), (0, 0)))
            proj, z = _norm_matmul(xf, g_mix, gla_w_in_bf, j, n_main, wz=w_z)
            mix = _gla_core(proj, z, wg, gla_b_gate[j].reshape(1, -1),
                            0.5 * gla_g_out[j].reshape(1, d), batch, seq)
            w_out = gla_w_out_bf
        else:
            hd = att_g_q.shape[1]
            gh = jnp.concatenate([jnp.tile(att_g_q[j] * (hd ** -0.5 * LOG2E), ATT_HEADS),
                                  jnp.tile(att_g_k[j], ATT_HEADS),
                                  jnp.ones((d,), F32)]).reshape(1, -1)
            qkv = _norm_matmul(xf, g_mix, att_w_in_bf, j, 3 * d, gh=gh, hd=hd, n_norm=2 * d)
            mix = _attn_core(qkv, att_rel_bias[j], batch, seq)
            w_out = att_w_out_bf
        xf = _matmul_res(mix, w_out, j, xf)
        xf = _mlp(xf, norm_mlp_g[i].reshape(1, d), mlp_w_up_bf, mlp_w_down_bf, i)
    return xf.reshape(batch, seq, d)
```

```python
import functools

import jax
import jax.numpy as jnp
from jax import lax
from jax.experimental import pallas as pl
from jax.experimental.pallas import tpu as pltpu

F32 = jnp.float32
BF16 = jnp.bfloat16

EPS = 1e-6
CHUNK = 64
N_MIXERS = 2
GLA_HEADS = 4
GLA_GATE_RANK = 16
GLA_TAU = 16.0
ATT_HEADS = 16
LEFT_CHUNKS = 8
REL_CLIP = 256

LANES = 128
NORM_ROWS = 16
NORM_UNROLL = 8
HEADNORM_SPLIT = 4
MLP_SPLIT = 2
VMEM_LIMIT = 56 * 1024 * 1024
NEG = -1e30
LOG2E = 1.4426950408889634

GLA_ROWS = 256
GLA_HB = 4
ATT_QB = 256
ATT_KB = ATT_QB + LEFT_CHUNKS * CHUNK
ATT_HB = 16
ATT_QSUB = 2
ATT_INTERLEAVE = 1


def _params(*sem):
    return pltpu.CompilerParams(dimension_semantics=sem, vmem_limit_bytes=VMEM_LIMIT)


def _interleave(chains, width):
    for start in range(0, len(chains), width):
        group = chains[start:start + width]
        while group:
            alive = []
            for chain in group:
                try:
                    next(chain)
                    alive.append(chain)
                except StopIteration:
                    pass
            group = alive


def _rmsnorm_rows_to(x_ref, g_ref, dst_ref, copy_ref=None):
    rows = x_ref.shape[0]

    def body(r, carry):
        sl = pl.ds(pl.multiple_of(r * NORM_ROWS, NORM_ROWS), NORM_ROWS)
        x = x_ref[sl, :]
        ms = jnp.mean(x * x, axis=-1, keepdims=True)
        dst_ref[sl, :] = ((x * lax.rsqrt(ms + EPS)) * g_ref[...]).astype(dst_ref.dtype)
        if copy_ref is not None:
            copy_ref[sl, :] = x
        return carry

    lax.fori_loop(0, rows // NORM_ROWS, body, 0, unroll=NORM_UNROLL)


def _rmsnorm_rows_static(x_ref, g_ref, dst_ref, rs, copy_ref=None):
    for r0 in range(rs.start, rs.stop, NORM_ROWS):
        sl = slice(r0, r0 + NORM_ROWS)
        x = x_ref[sl, :]
        ms = jnp.mean(x * x, axis=-1, keepdims=True)
        dst_ref[sl, :] = ((x * lax.rsqrt(ms + EPS)) * g_ref[...]).astype(dst_ref.dtype)
        if copy_ref is not None:
            copy_ref[sl, :] = x


def _norm_matmul_headnorm_kernel(x_ref, g_ref, w_ref, gh_ref, o_ref, xn_ref, *, norm_blocks, hd):
    j = pl.program_id(1)
    normed = j < norm_blocks
    rows = xn_ref.shape[0] // HEADNORM_SPLIT

    def step(first):
        for part in range(HEADNORM_SPLIT):
            rs = slice(part * rows, (part + 1) * rows)
            if first:
                _rmsnorm_rows_static(x_ref, g_ref, xn_ref, rs)
            acc = jnp.dot(xn_ref[rs, :], w_ref[...], preferred_element_type=F32)
            for h in range(acc.shape[1] // hd):
                cs = slice(h * hd, (h + 1) * hd)
                a = acc[:, cs]
                ms = jnp.mean(a * a, axis=-1, keepdims=True)
                scale = jnp.where(normed, lax.rsqrt(ms + EPS), 1.0)
                o_ref[rs, cs] = ((a * scale) * gh_ref[:, cs]).astype(o_ref.dtype)

    pl.when(j == 0)(functools.partial(step, True))
    pl.when(j != 0)(functools.partial(step, False))


def _norm_matmul_z_kernel(x_ref, g_ref, w_ref, wz_ref, o_ref, z_ref, xn_ref):
    rows = xn_ref.shape[0] // HEADNORM_SPLIT

    def step(first):
        for part in range(HEADNORM_SPLIT):
            rs = slice(part * rows, (part + 1) * rows)
            if first:
                _rmsnorm_rows_static(x_ref, g_ref, xn_ref, rs)
            o_ref[rs, :] = jnp.dot(xn_ref[rs, :], w_ref[...],
                                   preferred_element_type=F32).astype(o_ref.dtype)
        if first:
            z_ref[...] = jnp.dot(xn_ref[...], wz_ref[...], preferred_element_type=F32)

    j = pl.program_id(1)
    pl.when(j == 0)(functools.partial(step, True))
    pl.when(j != 0)(functools.partial(step, False))


def _norm_matmul(x, g, w, layer, n, wz=None, gh=None, hd=None, n_norm=0, *, tm=1024, tn=2048):
    t, d = x.shape
    tm, tn = min(tm, t), min(tn, n)
    grid = (t // tm, n // tn)
    x_spec = pl.BlockSpec((tm, d), lambda i, j: (i, 0))
    g_spec = pl.BlockSpec((1, d), lambda i, j: (0, 0))
    w_spec = pl.BlockSpec((None, d, tn), lambda i, j: (layer, 0, j))
    o_spec = pl.BlockSpec((tm, tn), lambda i, j: (i, j))
    scratch = [pltpu.VMEM((tm, d), BF16)]
    if gh is not None:
        assert n_norm % tn == 0 and tn % hd == 0 and gh.shape[1] == n
        gh_spec = pl.BlockSpec((1, tn), lambda i, j: (0, j))
        return pl.pallas_call(
            functools.partial(_norm_matmul_headnorm_kernel, norm_blocks=n_norm // tn, hd=hd),
            out_shape=jax.ShapeDtypeStruct((t, n), BF16),
            grid=grid, in_specs=[x_spec, g_spec, w_spec, gh_spec], out_specs=o_spec,
            scratch_shapes=scratch, compiler_params=_params("parallel", "arbitrary"),
            name="norm_matmul_headnorm",
        )(x, g, w, gh)
    nz = wz.shape[1]
    wz_spec = pl.BlockSpec((d, nz), lambda i, j: (0, 0))
    z_spec = pl.BlockSpec((tm, nz), lambda i, j: (i, 0))
    return pl.pallas_call(
        _norm_matmul_z_kernel,
        out_shape=(jax.ShapeDtypeStruct((t, n), BF16), jax.ShapeDtypeStruct((t, nz), F32)),
        grid=grid, in_specs=[x_spec, g_spec, w_spec, wz_spec], out_specs=(o_spec, z_spec),
        scratch_shapes=scratch, compiler_params=_params("parallel", "arbitrary"),
        name="norm_matmul_z",
    )(x, g, w, wz)


def _matmul_res_kernel(a_ref, w_ref, r_ref, o_ref):
    o_ref[...] = r_ref[...] + jnp.dot(a_ref[...], w_ref[...], preferred_element_type=F32)


def _matmul_res(a, w, layer, res, *, tm=512, tn=2048):
    t, k = a.shape
    n = w.shape[2]
    tm, tn = min(tm, t), min(tn, n)
    return pl.pallas_call(
        _matmul_res_kernel,
        out_shape=jax.ShapeDtypeStruct((t, n), F32),
        grid=(t // tm, n // tn),
        in_specs=[pl.BlockSpec((tm, k), lambda i, j: (i, 0)),
                  pl.BlockSpec((None, k, tn), lambda i, j: (layer, 0, j)),
                  pl.BlockSpec((tm, tn), lambda i, j: (i, j))],
        out_specs=pl.BlockSpec((tm, tn), lambda i, j: (i, j)),
        compiler_params=_params("parallel", "parallel"),
        name="matmul_res",
    )(a, w, res)


def _mlp_kernel(x_ref, g_ref, wu_ref, wd_ref, o_ref, xn_ref):
    rows = xn_ref.shape[0] // MLP_SPLIT

    def step(first):
        for part in range(MLP_SPLIT):
            rs = slice(part * rows, (part + 1) * rows)
            if first:
                _rmsnorm_rows_static(x_ref, g_ref, xn_ref, rs)
            h = jnp.dot(xn_ref[rs, :], wu_ref[...], preferred_element_type=F32)
            h = jnp.square(jnp.maximum(h, 0.0)).astype(BF16)
            y = jnp.dot(h, wd_ref[...], preferred_element_type=F32)
            o_ref[rs, :] = (x_ref[rs, :] if first else o_ref[rs, :]) + y

    f = pl.program_id(1)
    pl.when(f == 0)(functools.partial(step, True))
    pl.when(f != 0)(functools.partial(step, False))


def _mlp(x, g, w_up, w_down, layer, *, tm=512, tf=2048):
    t, d = x.shape
    f = w_up.shape[2]
    tm, tf = min(tm, t), min(tf, f)
    return pl.pallas_call(
        _mlp_kernel,
        out_shape=jax.ShapeDtypeStruct((t, d), F32),
        grid=(t // tm, f // tf),
        in_specs=[pl.BlockSpec((tm, d), lambda i, j: (i, 0)),
                  pl.BlockSpec((1, d), lambda i, j: (0, 0)),
                  pl.BlockSpec((None, d, tf), lambda i, j: (layer, 0, j)),
                  pl.BlockSpec((None, tf, d), lambda i, j: (layer, j, 0))],
        out_specs=pl.BlockSpec((tm, d), lambda i, j: (i, 0)),
        scratch_shapes=[pltpu.VMEM((tm, d), BF16)],
        compiler_params=_params("parallel", "arbitrary"),
        name="mlp",
    )(x, g, w_up, w_down)


def _split_bf16(a):
    hi = a.astype(BF16)
    lo = (a - hi.astype(F32)).astype(BF16)
    return hi, lo


def _dot(a, b):
    return jnp.dot(a, b, preferred_element_type=F32)


def _dot_tn(a, b):
    return lax.dot_general(a, b, (((0,), (0,)), ((), ())), preferred_element_type=F32)


def _dot_nt(a, b):
    return lax.dot_general(a, b, (((1,), (1,)), ((), ())), preferred_element_type=F32)


def _gla_kernel(q_ref, k_ref, v_ref, r_ref, z_ref, wgh_ref, wgl_ref, bg_ref, go_ref, tri_ref,
                o_ref, state_ref):
    hb = state_ref.shape[0]

    @pl.when(pl.program_id(2) == 0)
    def _():
        state_ref[...] = jnp.zeros_like(state_ref)

    z_parts = _split_bf16(z_ref[...])
    tri = tri_ref[...]
    _interleave([_gla_head(h, q_ref, k_ref, v_ref, r_ref, z_parts, wgh_ref, wgl_ref, bg_ref, go_ref,
                           tri, o_ref, state_ref) for h in range(hb)], hb)


def _gla_head(h, q_ref, k_ref, v_ref, r_ref, z_parts, wgh_ref, wgl_ref, bg_ref, go_ref, tri,
              o_ref, state_ref):
    rows = q_ref.shape[0]
    hb = state_ref.shape[0]
    dk = q_ref.shape[1] // hb
    dv = v_ref.shape[1] // hb
    ks = slice(h * dk, (h + 1) * dk)
    vs = slice(h * dv, (h + 1) * dv)
    nc = rows // CHUNK
    chunk = lambda c: slice(c * CHUNK, (c + 1) * CHUNK)

    z_hi, z_lo = z_parts
    wg_hi = wgh_ref[:, ks]
    logit = _dot(z_hi, wg_hi) + _dot(z_lo, wg_hi) + _dot(z_hi, wgl_ref[:, ks]) + bg_ref[:, ks]
    t = logit * LOG2E
    yield
    log_a = (jnp.minimum(t, 0.0) - jnp.log2(1.0 + jnp.exp2(-jnp.abs(t)))) * (1.0 / GLA_TAU)

    la_hi, la_lo = _split_bf16(log_a)
    b = _dot(tri, la_hi) + _dot(tri, la_lo)
    yield
    bl = [b[(c + 1) * CHUNK - 1:(c + 1) * CHUNK, :] for c in range(nc)]

    def dec(c1, c2):
        return jnp.exp2(functools.reduce(jnp.add, bl[c1:c2]))

    q_dec = q_ref[:, ks].astype(F32) * (dk ** -0.5) * jnp.exp2(b)
    k = k_ref[:, ks].astype(F32)
    k_inv = (k * jnp.exp2(-b)).astype(BF16)
    k_end = [k[chunk(c)] * jnp.exp2(bl[c] - b[chunk(c)]) for c in range(nc)]
    q_dec_bf = q_dec.astype(BF16)
    yield

    trow = lax.broadcasted_iota(jnp.int32, (CHUNK, rows), 0)
    scol = lax.broadcasted_iota(jnp.int32, (CHUNK, rows), 1)
    a_rows = []
    for c in range(nc):
        keys = [(k_end[cp] * dec(cp + 1, c) if cp + 1 < c else k_end[cp]).astype(BF16)
                for cp in range(c)]
        keys.append(k_inv[chunk(c)])
        if c + 1 < nc:
            keys.append(jnp.zeros(((nc - 1 - c) * CHUNK, dk), BF16))
        s = _dot_nt(q_dec_bf[chunk(c)], jnp.concatenate(keys, axis=0))
        a_rows.append(jnp.where(scol <= trow + c * CHUNK, s, 0.0).astype(BF16))
    a = jnp.concatenate(a_rows, axis=0)
    yield

    q_in = jnp.concatenate(
        [(q_dec[chunk(c)] * dec(0, c)).astype(BF16) if c else q_dec_bf[chunk(0)] for c in range(nc)],
        axis=0)
    k_out = jnp.concatenate(
        [(k_end[c] * dec(c + 1, nc) if c + 1 < nc else k_end[c]).astype(BF16) for c in range(nc)],
        axis=0)

    v = v_ref[:, vs]
    state = state_ref[h]
    o = _dot(a, v) + _dot_nt(q_in, state.astype(BF16))
    state_ref[h] = state * dec(0, nc) + _dot_tn(v, k_out)
    yield

    for c in range(nc):
        oc = o[chunk(c)]
        ms = jnp.mean(oc * oc, axis=-1, keepdims=True)
        r = r_ref[chunk(c), vs].astype(F32)
        y = (oc * lax.rsqrt(ms + EPS)) * go_ref[:, vs]
        o_ref[chunk(c), vs] = (y * (r * (jnp.tanh(0.5 * r) + 1.0))).astype(o_ref.dtype)


def _gla_core(proj, z, wg, bg, g_half, batch, seq):
    t = proj.shape[0]
    d = g_half.shape[1]
    dv = d // GLA_HEADS
    dk = dv // 2
    rows = min(GLA_ROWS, seq)
    nblk = seq // rows
    hb = GLA_HB
    ng = GLA_HEADS // hb
    wk, wv = hb * dk, hb * dv
    k_off = ng
    v_off = (2 * GLA_HEADS * dk) // wv
    r_off = v_off + ng
    assert (2 * GLA_HEADS * dk) % wv == 0
    wg_hi, wg_lo = _split_bf16(wg)
    idx = jnp.arange(rows)
    tri = ((idx[:, None] // CHUNK == idx[None, :] // CHUNK) & (idx[None, :] <= idx[:, None])).astype(BF16)
    rowmap = lambda b, h, n: b * nblk + n
    wg_spec = pl.BlockSpec((wg.shape[0], wk), lambda b, h, n: (0, h))
    return pl.pallas_call(
        _gla_kernel,
        out_shape=jax.ShapeDtypeStruct((t, d), BF16),
        grid=(batch, ng, nblk),
        in_specs=[pl.BlockSpec((rows, wk), lambda b, h, n: (rowmap(b, h, n), h)),
                  pl.BlockSpec((rows, wk), lambda b, h, n: (rowmap(b, h, n), k_off + h)),
                  pl.BlockSpec((rows, wv), lambda b, h, n: (rowmap(b, h, n), v_off + h)),
                  pl.BlockSpec((rows, wv), lambda b, h, n: (rowmap(b, h, n), r_off + h)),
                  pl.BlockSpec((rows, z.shape[1]), lambda b, h, n: (rowmap(b, h, n), 0)),
                  wg_spec, wg_spec,
                  pl.BlockSpec((1, wk), lambda b, h, n: (0, h)),
                  pl.BlockSpec((1, wv), lambda b, h, n: (0, h)),
                  pl.BlockSpec((rows, rows), lambda b, h, n: (0, 0))],
        out_specs=pl.BlockSpec((rows, wv), lambda b, h, n: (rowmap(b, h, n), h)),
        scratch_shapes=[pltpu.VMEM((hb, dv, dk), F32)],
        compiler_params=_params("parallel", "parallel", "arbitrary"),
        name="gla_core",
    )(proj, proj, proj, proj, z, wg_hi, wg_lo, bg, g_half, tri)


def _attn_kernel(q_ref, *refs):
    qb = ATT_QB
    nkb = ATT_KB // qb
    nk = nkb + ATT_QSUB - 1
    k_refs, v_refs = refs[:nk], refs[nk:2 * nk]
    gen_ref, o_ref, bias_ref = refs[2 * nk:]
    hb = bias_ref.shape[0]
    hd = q_ref.shape[1] // hb
    cpb = qb // CHUNK
    i = pl.program_id(2)

    @pl.when((pl.program_id(1) == 0) & (i == 0))
    def _():
        cq = lax.broadcasted_iota(jnp.int32, (qb, qb), 0) // CHUNK
        ck = lax.broadcasted_iota(jnp.int32, (qb, qb), 1) // CHUNK
        for j in range(nkb):
            band = (ck + cpb * j >= cq) & (ck + cpb * j <= cq + LEFT_CHUNKS)
            for h in range(hb):
                gen = jnp.broadcast_to(gen_ref[h, j:j + 1, :], (qb, 2 * qb))
                toe = pltpu.roll(gen, qb + 1, 1, stride=1, stride_axis=0)[:, :qb]
                bias_ref[h, :, j * qb:(j + 1) * qb] = jnp.where(band, toe, NEG)

    ones = jnp.ones((qb, hd), BF16)
    q_ext = jnp.full((qb, hd), 1.0 / hd, BF16)
    k_ext = [jnp.concatenate(
        [jnp.full((qb, hd), jnp.where(ATT_QSUB * i + u + j >= nkb - 1, 0.0, NEG), F32).astype(BF16)
         for j in range(nkb)], axis=0) for u in range(ATT_QSUB)]

    def head(h, u):
        cs = slice(h * hd, (h + 1) * hd)
        qs = slice(u * qb, (u + 1) * qb)
        kr, vr = k_refs[u:u + nkb], v_refs[u:u + nkb]
        q = jnp.concatenate([q_ref[qs, cs], q_ext], axis=1)
        k = jnp.concatenate([jnp.concatenate([r[:, cs] for r in kr], axis=0), k_ext[u]], axis=1)
        s = _dot_nt(q, k) + bias_ref[h]
        yield
        m = jnp.max(s, axis=-1, keepdims=True)
        yield
        p = jnp.exp2(s - m).astype(BF16)
        yield
        v1 = jnp.concatenate([jnp.concatenate([r[:, cs] for r in vr], axis=0),
                              jnp.concatenate([ones] * nkb, axis=0)], axis=1)
        acc = _dot(p, v1)
        yield
        o_ref[qs, cs] = (acc[:, :hd] / acc[:, hd:]).astype(o_ref.dtype)

    _interleave([head(h, u) for u in range(ATT_QSUB) for h in range(hb)], ATT_INTERLEAVE)


def _attn_bias_generators(rel_bias, qb):
    left = LEFT_CHUNKS * CHUNK
    x = jnp.arange(2 * qb)[None, :]
    j = jnp.arange((qb + left) // qb)[:, None]
    dist = left - qb * j + (qb - 1) - x
    idx = jnp.clip(dist, -(CHUNK - 1), REL_CLIP) + (CHUNK - 1)
    return rel_bias[:, idx].astype(F32) * LOG2E


def _attn_core(qkv, rel_bias, batch, seq):
    t = qkv.shape[0]
    d = qkv.shape[1] // 3
    hd = d // ATT_HEADS
    qb = ATT_QB
    nkb = ATT_KB // qb
    nsub = ATT_QSUB
    nk = nkb + nsub - 1
    assert seq % (nsub * qb) == 0 and ATT_KB % qb == 0
    nq = seq // qb
    hb = ATT_HB
    ng = ATT_HEADS // hb
    gen = _attn_bias_generators(rel_bias, qb)
    w = hb * hd

    def kv_spec(off, t):
        return pl.BlockSpec(
            (qb, w), lambda g, b, i: (b * nq + jnp.maximum(nsub * i - (nkb - 1) + t, 0), off * ng + g))

    qo_spec = pl.BlockSpec((nsub * qb, w), lambda g, b, i: (b * (nq // nsub) + i, g))

    return pl.pallas_call(
        _attn_kernel,
        out_shape=jax.ShapeDtypeStruct((t, d), BF16),
        grid=(ng, batch, nq // nsub),
        in_specs=[qo_spec] + [kv_spec(1, t) for t in range(nk)] + [kv_spec(2, t) for t in range(nk)]
        + [pl.BlockSpec((hb,) + gen.shape[1:], lambda g, b, i: (g, 0, 0))],
        out_specs=qo_spec,
        scratch_shapes=[pltpu.VMEM((hb, qb, ATT_KB), F32)],
        compiler_params=_params("arbitrary", "arbitrary", "arbitrary"),
        name="attn_core",
    )(*([qkv] * (1 + 2 * nk)), gen)


def kernel(x, norm_mix_g, norm_mlp_g, gla_w_in, gla_w_gate_up, gla_b_gate, gla_g_out, gla_w_out,
           att_w_in, att_g_q, att_g_k, att_rel_bias, att_w_out, mlp_w_up, mlp_w_down):
    batch, seq, d = x.shape
    depth = norm_mix_g.shape[0]
    xf = x.reshape(batch * seq, d)
    gla_w_in_bf, gla_w_out_bf = gla_w_in.astype(BF16), gla_w_out.astype(BF16)
    att_w_in_bf, att_w_out_bf = att_w_in.astype(BF16), att_w_out.astype(BF16)
    mlp_w_up_bf, mlp_w_down_bf = mlp_w_up.astype(BF16), mlp_w_down.astype(BF16)
    for i in range(depth):
        j = i // N_MIXERS
        g_mix = norm_mix_g[i].reshape(1, d)
        if i % N_MIXERS == 0:
            n_main = gla_w_in.shape[2] - GLA_GATE_RANK
            w_z = jnp.pad(gla_w_in[j, :, n_main:], ((0, 0), (0, LANES - GLA_GATE_RANK))).astype(BF16)
            wg = jnp.pad(gla_w_gate_up[j], ((0, LANES - GLA_GATE_RANK), (0, 0)))
            proj, z = _norm_matmul(xf, g_mix, gla_w_in_bf, j, n_main, wz=w_z)
            mix = _gla_core(proj, z, wg, gla_b_gate[j].reshape(1, -1),
                            0.5 * gla_g_out[j].reshape(1, d), batch, seq)
            w_out = gla_w_out_bf
        else:
            hd = att_g_q.shape[1]
            gh = jnp.concatenate([jnp.tile(att_g_q[j] * (hd ** -0.5 * LOG2E), ATT_HEADS),
                                  jnp.tile(att_g_k[j], ATT_HEADS),
                                  jnp.ones((d,), F32)]).reshape(1, -1)
            qkv = _norm_matmul(xf, g_mix, att_w_in_bf, j, 3 * d, gh=gh, hd=hd, n_norm=2 * d)
            mix = _attn_core(qkv, att_rel_bias[j], batch, seq)
            w_out = att_w_out_bf
        xf = _matmul_res(mix, w_out, j, xf)
        xf = _mlp(xf, norm_mlp_g[i].reshape(1, d), mlp_w_up_bf, mlp_w_down_bf, i)
    return xf.reshape(batch, seq, d)
```

```python
import functools

import jax
import jax.numpy as jnp
from jax import lax
from jax.experimental import pallas as pl
from jax.experimental.pallas import tpu as pltpu

F32 = jnp.float32
BF16 = jnp.bfloat16

EPS = 1e-6
CHUNK = 64
N_MIXERS = 2
GLA_HEADS = 4
GLA_GATE_RANK = 16
GLA_TAU = 16.0
ATT_HEADS = 16
LEFT_CHUNKS = 8
REL_CLIP = 256

LANES = 128
NORM_ROWS = 16
HEADNORM_SPLIT = 4
MLP_SPLIT = 2
VMEM_LIMIT = 56 * 1024 * 1024
NEG = -1e30
LOG2E = 1.4426950408889634

GLA_ROWS = 256
GLA_HB = 4
ATT_QB = 256
ATT_KB = ATT_QB + LEFT_CHUNKS * CHUNK
ATT_HB = 16
ATT_QSUB = 2
ATT_INTERLEAVE = 1


def _params(*sem):
    return pltpu.CompilerParams(dimension_semantics=sem, vmem_limit_bytes=VMEM_LIMIT)


def _interleave(chains, width):
    for start in range(0, len(chains), width):
        group = chains[start:start + width]
        while group:
            alive = []
            for chain in group:
                try:
                    next(chain)
                    alive.append(chain)
                except StopIteration:
                    pass
            group = alive


def _rmsnorm_rows_static(x_ref, g_ref, dst_ref, rs):
    for r0 in range(rs.start, rs.stop, NORM_ROWS):
        sl = slice(r0, r0 + NORM_ROWS)
        x = x_ref[sl, :]
        ms = jnp.mean(x * x, axis=-1, keepdims=True)
        dst_ref[sl, :] = ((x * lax.rsqrt(ms + EPS)) * g_ref[...]).astype(dst_ref.dtype)


def _norm_matmul_headnorm_kernel(x_ref, g_ref, w_ref, gh_ref, o_ref, xn_ref, *, norm_blocks, hd):
    j = pl.program_id(1)
    normed = j < norm_blocks
    rows = xn_ref.shape[0] // HEADNORM_SPLIT

    def step(first):
        for part in range(HEADNORM_SPLIT):
            rs = slice(part * rows, (part + 1) * rows)
            if first:
                _rmsnorm_rows_static(x_ref, g_ref, xn_ref, rs)
            acc = jnp.dot(xn_ref[rs, :], w_ref[...], preferred_element_type=F32)
            for h in range(acc.shape[1] // hd):
                cs = slice(h * hd, (h + 1) * hd)
                a = acc[:, cs]
                ms = jnp.mean(a * a, axis=-1, keepdims=True)
                scale = jnp.where(normed, lax.rsqrt(ms + EPS), 1.0)
                o_ref[rs, cs] = ((a * scale) * gh_ref[:, cs]).astype(o_ref.dtype)

    pl.when(j == 0)(functools.partial(step, True))
    pl.when(j != 0)(functools.partial(step, False))


def _norm_matmul_z_kernel(x_ref, g_ref, w_ref, wz_ref, o_ref, z_ref, xn_ref):
    rows = xn_ref.shape[0] // HEADNORM_SPLIT

    def step(first):
        for part in range(HEADNORM_SPLIT):
            rs = slice(part * rows, (part + 1) * rows)
            if first:
                _rmsnorm_rows_static(x_ref, g_ref, xn_ref, rs)
            o_ref[rs, :] = jnp.dot(xn_ref[rs, :], w_ref[...],
                                   preferred_element_type=F32).astype(o_ref.dtype)
        if first:
            z_ref[...] = jnp.dot(xn_ref[...], wz_ref[...], preferred_element_type=F32)

    j = pl.program_id(1)
    pl.when(j == 0)(functools.partial(step, True))
    pl.when(j != 0)(functools.partial(step, False))


def _norm_matmul(x, g, w, layer, n, wz=None, gh=None, hd=None, n_norm=0, *, tm=1024, tn=2048):
    t, d = x.shape
    tm, tn = min(tm, t), min(tn, n)
    grid = (t // tm, n // tn)
    x_spec = pl.BlockSpec((tm, d), lambda i, j: (i, 0))
    g_spec = pl.BlockSpec((1, d), lambda i, j: (0, 0))
    w_spec = pl.BlockSpec((None, d, tn), lambda i, j: (layer, 0, j))
    o_spec = pl.BlockSpec((tm, tn), lambda i, j: (i, j))
    scratch = [pltpu.VMEM((tm, d), BF16)]
    if gh is not None:
        assert n_norm % tn == 0 and tn % hd == 0 and gh.shape[1] == n
        gh_spec = pl.BlockSpec((1, tn), lambda i, j: (0, j))
        return pl.pallas_call(
            functools.partial(_norm_matmul_headnorm_kernel, norm_blocks=n_norm // tn, hd=hd),
            out_shape=jax.ShapeDtypeStruct((t, n), BF16),
            grid=grid, in_specs=[x_spec, g_spec, w_spec, gh_spec], out_specs=o_spec,
            scratch_shapes=scratch, compiler_params=_params("parallel", "arbitrary"),
            name="norm_matmul_headnorm",
        )(x, g, w, gh)
    nz = wz.shape[1]
    wz_spec = pl.BlockSpec((d, nz), lambda i, j: (0, 0))
    z_spec = pl.BlockSpec((tm, nz), lambda i, j: (i, 0))
    return pl.pallas_call(
        _norm_matmul_z_kernel,
        out_shape=(jax.ShapeDtypeStruct((t, n), BF16), jax.ShapeDtypeStruct((t, nz), F32)),
        grid=grid, in_specs=[x_spec, g_spec, w_spec, wz_spec], out_specs=(o_spec, z_spec),
        scratch_shapes=scratch, compiler_params=_params("parallel", "arbitrary"),
        name="norm_matmul_z",
    )(x, g, w, wz)


def _matmul_res_kernel(a_ref, w_ref, r_ref, o_ref):
    o_ref[...] = r_ref[...] + jnp.dot(a_ref[...], w_ref[...], preferred_element_type=F32)


def _matmul_res(a, w, layer, res, *, tm=512, tn=2048):
    t, k = a.shape
    n = w.shape[2]
    tm, tn = min(tm, t), min(tn, n)
    return pl.pallas_call(
        _matmul_res_kernel,
        out_shape=jax.ShapeDtypeStruct((t, n), F32),
        grid=(t // tm, n // tn),
        in_specs=[pl.BlockSpec((tm, k), lambda i, j: (i, 0)),
                  pl.BlockSpec((None, k, tn), lambda i, j: (layer, 0, j)),
                  pl.BlockSpec((tm, tn), lambda i, j: (i, j))],
        out_specs=pl.BlockSpec((tm, tn), lambda i, j: (i, j)),
        compiler_params=_params("parallel", "parallel"),
        name="matmul_res",
    )(a, w, res)


def _mlp_kernel(x_ref, g_ref, wu_ref, wd_ref, o_ref, xn_ref):
    rows = xn_ref.shape[0] // MLP_SPLIT

    def step(first):
        for part in range(MLP_SPLIT):
            rs = slice(part * rows, (part + 1) * rows)
            if first:
                _rmsnorm_rows_static(x_ref, g_ref, xn_ref, rs)
            h = jnp.dot(xn_ref[rs, :], wu_ref[...], preferred_element_type=F32)
            h = jnp.square(jnp.maximum(h, 0.0)).astype(BF16)
            y = jnp.dot(h, wd_ref[...], preferred_element_type=F32)
            o_ref[rs, :] = (x_ref[rs, :] if first else o_ref[rs, :]) + y

    f = pl.program_id(1)
    pl.when(f == 0)(functools.partial(step, True))
    pl.when(f != 0)(functools.partial(step, False))


def _mlp(x, g, w_up, w_down, layer, *, tm=512, tf=2048):
    t, d = x.shape
    f = w_up.shape[2]
    tm, tf = min(tm, t), min(tf, f)
    return pl.pallas_call(
        _mlp_kernel,
        out_shape=jax.ShapeDtypeStruct((t, d), F32),
        grid=(t // tm, f // tf),
        in_specs=[pl.BlockSpec((tm, d), lambda i, j: (i, 0)),
                  pl.BlockSpec((1, d), lambda i, j: (0, 0)),
                  pl.BlockSpec((None, d, tf), lambda i, j: (layer, 0, j)),
                  pl.BlockSpec((None, tf, d), lambda i, j: (layer, j, 0))],
        out_specs=pl.BlockSpec((tm, d), lambda i, j: (i, 0)),
        scratch_shapes=[pltpu.VMEM((tm, d), BF16)],
        compiler_params=_params("parallel", "arbitrary"),
        name="mlp",
    )(x, g, w_up, w_down)


def _split_bf16(a):
    hi = a.astype(BF16)
    lo = (a - hi.astype(F32)).astype(BF16)
    return hi, lo


def _dot(a, b):
    return jnp.dot(a, b, preferred_element_type=F32)


def _dot_tn(a, b):
    return lax.dot_general(a, b, (((0,), (0,)), ((), ())), preferred_element_type=F32)


def _dot_nt(a, b):
    return lax.dot_general(a, b, (((1,), (1,)), ((), ())), preferred_element_type=F32)


def _gla_kernel(q_ref, k_ref, v_ref, r_ref, z_ref, wgh_ref, wgl_ref, bg_ref, go_ref, tri_ref,
                o_ref, state_ref):
    hb = state_ref.shape[0]

    @pl.when(pl.program_id(2) == 0)
    def _():
        state_ref[...] = jnp.zeros_like(state_ref)

    z_parts = _split_bf16(z_ref[...])
    tri = tri_ref[...]
    _interleave([_gla_head(h, q_ref, k_ref, v_ref, r_ref, z_parts, wgh_ref, wgl_ref, bg_ref, go_ref,
                           tri, o_ref, state_ref) for h in range(hb)], hb)


def _gla_head(h, q_ref, k_ref, v_ref, r_ref, z_parts, wgh_ref, wgl_ref, bg_ref, go_ref, tri,
              o_ref, state_ref):
    rows = q_ref.shape[0]
    hb = state_ref.shape[0]
    dk = q_ref.shape[1] // hb
    dv = v_ref.shape[1] // hb
    ks = slice(h * dk, (h + 1) * dk)
    vs = slice(h * dv, (h + 1) * dv)
    nc = rows // CHUNK
    chunk = lambda c: slice(c * CHUNK, (c + 1) * CHUNK)

    z_hi, z_lo = z_parts
    wg_hi = wgh_ref[:, ks]
    logit = _dot(z_hi, wg_hi) + _dot(z_lo, wg_hi) + _dot(z_hi, wgl_ref[:, ks]) + bg_ref[:, ks]
    t = logit * LOG2E
    yield
    log_a = (jnp.minimum(t, 0.0) - jnp.log2(1.0 + jnp.exp2(-jnp.abs(t)))) * (1.0 / GLA_TAU)

    la_hi, la_lo = _split_bf16(log_a)
    b = _dot(tri, la_hi) + _dot(tri, la_lo)
    yield
    bl = [b[(c + 1) * CHUNK - 1:(c + 1) * CHUNK, :] for c in range(nc)]

    def dec(c1, c2):
        return jnp.exp2(functools.reduce(jnp.add, bl[c1:c2]))

    q_dec = q_ref[:, ks].astype(F32) * (dk ** -0.5) * jnp.exp2(b)
    k = k_ref[:, ks].astype(F32)
    k_inv = (k * jnp.exp2(-b)).astype(BF16)
    k_end = [k[chunk(c)] * jnp.exp2(bl[c] - b[chunk(c)]) for c in range(nc)]
    q_dec_bf = q_dec.astype(BF16)
    yield

    trow = lax.broadcasted_iota(jnp.int32, (CHUNK, rows), 0)
    scol = lax.broadcasted_iota(jnp.int32, (CHUNK, rows), 1)
    a_rows = []
    for c in range(nc):
        keys = [(k_end[cp] * dec(cp + 1, c) if cp + 1 < c else k_end[cp]).astype(BF16)
                for cp in range(c)]
        keys.append(k_inv[chunk(c)])
        if c + 1 < nc:
            keys.append(jnp.zeros(((nc - 1 - c) * CHUNK, dk), BF16))
        s = _dot_nt(q_dec_bf[chunk(c)], jnp.concatenate(keys, axis=0))
        a_rows.append(jnp.where(scol <= trow + c * CHUNK, s, 0.0).astype(BF16))
    a = jnp.concatenate(a_rows, axis=0)
    yield

    q_in = jnp.concatenate(
        [(q_dec[chunk(c)] * dec(0, c)).astype(BF16) if c else q_dec_bf[chunk(0)] for c in range(nc)],
        axis=0)
    k_out = jnp.concatenate(
        [(k_end[c] * dec(c + 1, nc) if c + 1 < nc else k_end[c]).astype(BF16) for c in range(nc)],
        axis=0)

    v = v_ref[:, vs]
    state = state_ref[h]
    o = _dot(a, v) + _dot_nt(q_in, state.astype(BF16))
    state_ref[h] = state * dec(0, nc) + _dot_tn(v, k_out)
    yield

    for c in range(nc):
        oc = o[chunk(c)]
        ms = jnp.mean(oc * oc, axis=-1, keepdims=True)
        r = r_ref[chunk(c), vs].astype(F32)
        y = (oc * lax.rsqrt(ms + EPS)) * go_ref[:, vs]
        o_ref[chunk(c), vs] = (y * (r * (jnp.tanh(0.5 * r) + 1.0))).astype(o_ref.dtype)


def _gla_core(proj, z, wg, bg, g_half, batch, seq):
    t = proj.shape[0]
    d = g_half.shape[1]
    dv = d // GLA_HEADS
    dk = dv // 2
    rows = min(GLA_ROWS, seq)
    nblk = seq // rows
    hb = GLA_HB
    ng = GLA_HEADS // hb
    wk, wv = hb * dk, hb * dv
    k_off = ng
    v_off = (2 * GLA_HEADS * dk) // wv
    r_off = v_off + ng
    assert (2 * GLA_HEADS * dk) % wv == 0
    wg_hi, wg_lo = _split_bf16(wg)
    idx = jnp.arange(rows)
    tri = ((idx[:, None] // CHUNK == idx[None, :] // CHUNK) & (idx[None, :] <= idx[:, None])).astype(BF16)
    rowmap = lambda b, h, n: b * nblk + n
    wg_spec = pl.BlockSpec((wg.shape[0], wk), lambda b, h, n: (0, h))
    return pl.pallas_call(
        _gla_kernel,
        out_shape=jax.ShapeDtypeStruct((t, d), BF16),
        grid=(batch, ng, nblk),
        in_specs=[pl.BlockSpec((rows, wk), lambda b, h, n: (rowmap(b, h, n), h)),
                  pl.BlockSpec((rows, wk), lambda b, h, n: (rowmap(b, h, n), k_off + h)),
                  pl.BlockSpec((rows, wv), lambda b, h, n: (rowmap(b, h, n), v_off + h)),
                  pl.BlockSpec((rows, wv), lambda b, h, n: (rowmap(b, h, n), r_off + h)),
                  pl.BlockSpec((rows, z.shape[1]), lambda b, h, n: (rowmap(b, h, n), 0)),
                  wg_spec, wg_spec,
                  pl.BlockSpec((1, wk), lambda b, h, n: (0, h)),
                  pl.BlockSpec((1, wv), lambda b, h, n: (0, h)),
                  pl.BlockSpec((rows, rows), lambda b, h, n: (0, 0))],
        out_specs=pl.BlockSpec((rows, wv), lambda b, h, n: (rowmap(b, h, n), h)),
        scratch_shapes=[pltpu.VMEM((hb, dv, dk), F32)],
        compiler_params=_params("parallel", "parallel", "arbitrary"),
        name="gla_core",
    )(proj, proj, proj, proj, z, wg_hi, wg_lo, bg, g_half, tri)


def _attn_kernel(q_ref, *refs):
    qb = ATT_QB
    nkb = ATT_KB // qb
    nk = nkb + ATT_QSUB - 1
    k_refs, v_refs = refs[:nk], refs[nk:2 * nk]
    gen_ref, o_ref, bias_ref = refs[2 * nk:]
    hb = bias_ref.shape[0]
    hd = q_ref.shape[1] // hb
    cpb = qb // CHUNK
    i = pl.program_id(2)

    @pl.when((pl.program_id(1) == 0) & (i == 0))
    def _():
        cq = lax.broadcasted_iota(jnp.int32, (qb, qb), 0) // CHUNK
        ck = lax.broadcasted_iota(jnp.int32, (qb, qb), 1) // CHUNK
        for j in range(nkb):
            band = (ck + cpb * j >= cq) & (ck + cpb * j <= cq + LEFT_CHUNKS)
            for h in range(hb):
                gen = jnp.broadcast_to(gen_ref[h, j:j + 1, :], (qb, 2 * qb))
                toe = pltpu.roll(gen, qb + 1, 1, stride=1, stride_axis=0)[:, :qb]
                bias_ref[h, :, j * qb:(j + 1) * qb] = jnp.where(band, toe, NEG)

    ones = jnp.ones((qb, hd), BF16)
    q_ext = jnp.full((qb, hd), 1.0 / hd, BF16)
    k_ext = [jnp.concatenate(
        [jnp.full((qb, hd), jnp.where(ATT_QSUB * i + u + j >= nkb - 1, 0.0, NEG), F32).astype(BF16)
         for j in range(nkb)], axis=0) for u in range(ATT_QSUB)]

    def head(h, u):
        cs = slice(h * hd, (h + 1) * hd)
        qs = slice(u * qb, (u + 1) * qb)
        kr, vr = k_refs[u:u + nkb], v_refs[u:u + nkb]
        q = jnp.concatenate([q_ref[qs, cs], q_ext], axis=1)
        k = jnp.concatenate([jnp.concatenate([r[:, cs] for r in kr], axis=0), k_ext[u]], axis=1)
        s = _dot_nt(q, k) + bias_ref[h]
        yield
        m = jnp.max(s, axis=-1, keepdims=True)
        yield
        p = jnp.exp2(s - m).astype(BF16)
        yield
        v1 = jnp.concatenate([jnp.concatenate([r[:, cs] for r in vr], axis=0),
                              jnp.concatenate([ones] * nkb, axis=0)], axis=1)
        acc = _dot(p, v1)
        yield
        o_ref[qs, cs] = (acc[:, :hd] / acc[:, hd:]).astype(o_ref.dtype)

    _interleave([head(h, u) for u in range(ATT_QSUB) for h in range(hb)], ATT_INTERLEAVE)


def _attn_bias_generators(rel_bias, qb):
    left = LEFT_CHUNKS * CHUNK
    x = jnp.arange(2 * qb)[None, :]
    j = jnp.arange((qb + left) // qb)[:, None]
    dist = left - qb * j + (qb - 1) - x
    idx = jnp.clip(dist, -(CHUNK - 1), REL_CLIP) + (CHUNK - 1)
    return rel_bias[:, idx].astype(F32) * LOG2E


def _attn_core(qkv, rel_bias, batch, seq):
    t = qkv.shape[0]
    d = qkv.shape[1] // 3
    hd = d // ATT_HEADS
    qb = ATT_QB
    nkb = ATT_KB // qb
    nsub = ATT_QSUB
    nk = nkb + nsub - 1
    assert seq % (nsub * qb) == 0 and ATT_KB % qb == 0
    nq = seq // qb
    hb = ATT_HB
    ng = ATT_HEADS // hb
    gen = _attn_bias_generators(rel_bias, qb)
    w = hb * hd

    def kv_spec(off, t):
        return pl.BlockSpec(
            (qb, w), lambda g, b, i: (b * nq + jnp.maximum(nsub * i - (nkb - 1) + t, 0), off * ng + g))

    qo_spec = pl.BlockSpec((nsub * qb, w), lambda g, b, i: (b * (nq // nsub) + i, g))

    return pl.pallas_call(
        _attn_kernel,
        out_shape=jax.ShapeDtypeStruct((t, d), BF16),
        grid=(ng, batch, nq // nsub),
        in_specs=[qo_spec] + [kv_spec(1, t) for t in range(nk)] + [kv_spec(2, t) for t in range(nk)]
        + [pl.BlockSpec((hb,) + gen.shape[1:], lambda g, b, i: (g, 0, 0))],
        out_specs=qo_spec,
        scratch_shapes=[pltpu.VMEM((hb, qb, ATT_KB), F32)],
        compiler_params=_params("arbitrary", "arbitrary", "arbitrary"),
        name="attn_core",
    )(*([qkv] * (1 + 2 * nk)), gen)


def kernel(x, norm_mix_g, norm_mlp_g, gla_w_in, gla_w_gate_up, gla_b_gate, gla_g_out, gla_w_out,
           att_w_in, att_g_q, att_g_k, att_rel_bias, att_w_out, mlp_w_up, mlp_w_down):
    batch, seq, d = x.shape
    depth = norm_mix_g.shape[0]
    xf = x.reshape(batch * seq, d)
    gla_w_in_bf, gla_w_out_bf = gla_w_in.astype(BF16), gla_w_out.astype(BF16)
    att_w_in_bf, att_w_out_bf = att_w_in.astype(BF16), att_w_out.astype(BF16)
    mlp_w_up_bf, mlp_w_down_bf = mlp_w_up.astype(BF16), mlp_w_down.astype(BF16)
    for i in range(depth):
        j = i // N_MIXERS
        g_mix = norm_mix_g[i].reshape(1, d)
        if i % N_MIXERS == 0:
            n_main = gla_w_in.shape[2] - GLA_GATE_RANK
            w_z = jnp.pad(gla_w_in[j, :, n_main:], ((0, 0), (0, LANES - GLA_GATE_RANK))).astype(BF16)
            wg = jnp.pad(gla_w_gate_up[j], ((0, LANES - GLA_GATE_RANK), (0, 0)))
            proj, z = _norm_matmul(xf, g_mix, gla_w_in_bf, j, n_main, wz=w_z)
            mix = _gla_core(proj, z, wg, gla_b_gate[j].reshape(1, -1),
                            0.5 * gla_g_out[j].reshape(1, d), batch, seq)
            w_out = gla_w_out_bf
        else:
            hd = att_g_q.shape[1]
            gh = jnp.concatenate([jnp.tile(att_g_q[j] * (hd ** -0.5 * LOG2E), ATT_HEADS),
                                  jnp.tile(att_g_k[j], ATT_HEADS),
                                  jnp.ones((d,), F32)]).reshape(1, -1)
            qkv = _norm_matmul(xf, g_mix, att_w_in_bf, j, 3 * d, gh=gh, hd=hd, n_norm=2 * d)
            mix = _attn_core(qkv, att_rel_bias[j], batch, seq)
            w_out = att_w_out_bf
        xf = _matmul_res(mix, w_out, j, xf)
        xf = _mlp(xf, norm_mlp_g[i].reshape(1, d), mlp_w_up_bf, mlp_w_down_bf, i)
    return xf.reshape(batch, seq, d)
```
